```python
import jax, jax.numpy as jnp
from jax import lax
import numpy as np

D_MODEL = 2048
BATCH = 2
SEQ = 4096
DEPTH = 2
DEC_BATCH = 128
DEC_SEQ = 4
PAST_LEN = 8192
PAGE_SIZE = 128

N_FOX_LAYERS = (DEPTH + 1) // 2
N_MLA_LAYERS = DEPTH // 2
BLOCK_Q = 128
EPS = 1e-6

FOX_HEADS = 16
FOX_KV_HEADS = 4
FOX_HEAD_DIM = D_MODEL // FOX_HEADS
FOX_GROUP = FOX_HEADS // FOX_KV_HEADS
FOX_WIDTH = FOX_HEADS * FOX_HEAD_DIM
FOX_KV_WIDTH = FOX_KV_HEADS * FOX_HEAD_DIM
FOX_IN = 2 * FOX_WIDTH + 2 * FOX_KV_WIDTH + FOX_HEADS

MLA_HEADS = 16
MLA_NOPE = 128
MLA_ROPE = 64
MLA_V = 128
MLA_QK = MLA_NOPE + MLA_ROPE
Q_LORA = 512
KV_LORA = 512
MLA_WIDTH = MLA_HEADS * MLA_V
MLA_IN = Q_LORA + KV_LORA + MLA_ROPE + MLA_WIDTH
ROPE_THETA = 10000.0

kernel_name = "fox_mla_interleaved_hybrid_step"


def _rms_norm(x, g):
    xf = x.astype(jnp.float32)
    y = xf * lax.rsqrt(jnp.mean(xf * xf, axis=-1, keepdims=True) + EPS)
    return (y * g.astype(jnp.float32)).astype(x.dtype)


def _rope(x, pos):
    half = MLA_ROPE // 2
    inv = 1.0 / (ROPE_THETA ** (jnp.arange(half, dtype=jnp.float32) * (2.0 / MLA_ROPE)))
    ang = pos.astype(jnp.float32)[:, None] * inv[None, :]
    shape = (1, ang.shape[0]) + (1,) * (x.ndim - 3) + (half,)
    cos = jnp.cos(ang).reshape(shape)
    sin = jnp.sin(ang).reshape(shape)
    xf = x.astype(jnp.float32)
    x1, x2 = xf[..., :half], xf[..., half:]
    return jnp.concatenate([x1 * cos - x2 * sin, x1 * sin + x2 * cos], axis=-1).astype(x.dtype)


def _fox_project(h, w_in, b_f):
    B, S = h.shape[:2]
    proj = h @ w_in
    o1 = FOX_WIDTH
    o2 = o1 + FOX_KV_WIDTH
    o3 = o2 + FOX_KV_WIDTH
    o4 = o3 + FOX_HEADS
    q, k, v, f, z = jnp.split(proj, [o1, o2, o3, o4], axis=-1)
    q = q.reshape(B, S, FOX_KV_HEADS, FOX_GROUP, FOX_HEAD_DIM)
    k = k.reshape(B, S, FOX_KV_HEADS, FOX_HEAD_DIM)
    v = v.reshape(B, S, FOX_KV_HEADS, FOX_HEAD_DIM)
    logf = jax.nn.log_sigmoid((f + b_f).astype(jnp.float32))
    return q, k, v, logf, z


def _fox_attend(q, cq, q_pos, segs):
    B, Q = q.shape[:2]
    scale = FOX_HEAD_DIM ** -0.5
    cq4 = cq.reshape(B, Q, FOX_KV_HEADS, FOX_GROUP).transpose(0, 2, 3, 1)[..., None]
    scores = []
    for k, v, ck, k_pos in segs:
        s = jnp.einsum('bqkgd,bskd->bkgqs', q, k, preferred_element_type=jnp.float32) * scale
        ck4 = ck.reshape(B, -1, FOX_KV_HEADS, FOX_GROUP).transpose(0, 2, 3, 1)[:, :, :, None, :]
        s = s + (cq4 - ck4)
        mask = k_pos[None, :] <= q_pos[:, None]
        scores.append(jnp.where(mask, s, -jnp.inf))
    p = jax.nn.softmax(jnp.concatenate(scores, axis=-1), axis=-1)
    out = None
    off = 0
    for k, v, ck, k_pos in segs:
        n = k.shape[1]
        o = jnp.einsum('bkgqs,bskd->bqkgd', p[..., off:off + n].astype(v.dtype), v)
        out = o if out is None else out + o
        off += n
    return out.reshape(B, Q, FOX_WIDTH)


def _fox_prompt(h, w_in, b_f, w_out):
    q, k, v, logf, z = _fox_project(h, w_in, b_f)
    B, S = h.shape[:2]
    c = jnp.cumsum(logf, axis=1)
    pos = jnp.arange(S)

    def block(i):
        start = i * BLOCK_Q
        qb = lax.dynamic_slice_in_dim(q, start, BLOCK_Q, axis=1)
        cb = lax.dynamic_slice_in_dim(c, start, BLOCK_Q, axis=1)
        return _fox_attend(qb, cb, start + jnp.arange(BLOCK_Q), [(k, v, c, pos)])

    o = lax.map(block, jnp.arange(S // BLOCK_Q))
    o = jnp.moveaxis(o, 0, 1).reshape(B, S, FOX_WIDTH)
    y = (o * jax.nn.silu(z)) @ w_out
    return y, k, v, logf


def _fox_sample(h, k_pool, v_pool, lf_pool, page_table, w_in, b_f, w_out):
    q, k, v, logf, z = _fox_project(h, w_in, b_f)
    Bd, S = h.shape[:2]
    past_len = page_table.shape[1] * PAGE_SIZE
    past_k = k_pool[page_table].reshape(Bd, past_len, FOX_KV_HEADS, FOX_HEAD_DIM)
    past_v = v_pool[page_table].reshape(Bd, past_len, FOX_KV_HEADS, FOX_HEAD_DIM)
    past_lf = lf_pool[page_table].reshape(Bd, past_len, FOX_HEADS).astype(jnp.float32)
    c_past = jnp.cumsum(past_lf, axis=1)
    c_new = c_past[:, -1:, :] + jnp.cumsum(logf, axis=1)
    past_pos = jnp.arange(past_len)
    new_pos = past_len + jnp.arange(S)
    o = _fox_attend(q, c_new, new_pos,
                    [(past_k, past_v, c_past, past_pos), (k, v, c_new, new_pos)])
    y = (o * jax.nn.silu(z)) @ w_out
    return y, k, v, logf


def _mla_project(h, pos, w_in, q_norm, w_q_b, kv_norm, w_uk):
    proj = h @ w_in
    o1 = Q_LORA
    o2 = o1 + KV_LORA
    o3 = o2 + MLA_ROPE
    qa, kva, kr, z = jnp.split(proj, [o1, o2, o3], axis=-1)
    q = jnp.einsum('bsr,rhd->bshd', _rms_norm(qa, q_norm), w_q_b)
    q_nope, q_rope = q[..., :MLA_NOPE], q[..., MLA_NOPE:]
    q_rope = _rope(q_rope, pos)
    q_lat = jnp.einsum('bshd,rhd->bshr', q_nope, w_uk)
    ckv = _rms_norm(kva, kv_norm)
    krope = _rope(kr, pos)
    return q_lat, q_rope, ckv, krope, z


def _mla_attend(q_lat, q_rope, q_pos, segs):
    scale = MLA_QK ** -0.5
    scores = []
    for ckv, krope, k_pos in segs:
        s = (jnp.einsum('bqhr,bsr->bhqs', q_lat, ckv, preferred_element_type=jnp.float32)
             + jnp.einsum('bqhd,bsd->bhqs', q_rope, krope, preferred_element_type=jnp.float32)) * scale
        mask = k_pos[None, :] <= q_pos[:, None]
        scores.append(jnp.where(mask, s, -jnp.inf))
    p = jax.nn.softmax(jnp.concatenate(scores, axis=-1), axis=-1)
    out = None
    off = 0
    for ckv, krope, k_pos in segs:
        n = ckv.shape[1]
        o = jnp.einsum('bhqs,bsr->bqhr', p[..., off:off + n].astype(ckv.dtype), ckv)
        out = o if out is None else out + o
        off += n
    return out


def _mla_output(o_lat, z, w_uv, w_out):
    B, Q = o_lat.shape[:2]
    o = jnp.einsum('bqhr,rhd->bqhd', o_lat, w_uv).reshape(B, Q, MLA_WIDTH)
    return (o * jax.nn.silu(z)) @ w_out


def _mla_prompt(h, w_in, q_norm, w_q_b, kv_norm, w_uk, w_uv, w_out):
    B, S = h.shape[:2]
    pos = jnp.arange(S)
    q_lat, q_rope, ckv, krope, z = _mla_project(h, pos, w_in, q_norm, w_q_b, kv_norm, w_uk)

    def block(i):
        start = i * BLOCK_Q
        ql = lax.dynamic_slice_in_dim(q_lat, start, BLOCK_Q, axis=1)
        qr = lax.dynamic_slice_in_dim(q_rope, start, BLOCK_Q, axis=1)
        return _mla_attend(ql, qr, start + jnp.arange(BLOCK_Q), [(ckv, krope, pos)])

    o_lat = lax.map(block, jnp.arange(S // BLOCK_Q))
    o_lat = jnp.moveaxis(o_lat, 0, 1).reshape(B, S, MLA_HEADS, KV_LORA)
    return _mla_output(o_lat, z, w_uv, w_out), ckv, krope


def _mla_sample(h, ckv_pool, kr_pool, page_table, w_in, q_norm, w_q_b, kv_norm, w_uk, w_uv, w_out):
    Bd, S = h.shape[:2]
    past_len = page_table.shape[1] * PAGE_SIZE
    new_pos = past_len + jnp.arange(S)
    q_lat, q_rope, ckv, krope, z = _mla_project(h, new_pos, w_in, q_norm, w_q_b, kv_norm, w_uk)
    past_ckv = ckv_pool[page_table].reshape(Bd, past_len, KV_LORA)
    past_kr = kr_pool[page_table].reshape(Bd, past_len, MLA_ROPE)
    o_lat = _mla_attend(q_lat, q_rope, new_pos,
                        [(past_ckv, past_kr, jnp.arange(past_len)), (ckv, krope, new_pos)])
    return _mla_output(o_lat, z, w_uv, w_out), ckv, krope


def setup_inputs(seed: int = 0) -> dict:
    key = jax.random.key(seed)
    ks = jax.random.split(key, 24)
    n_pages = PAST_LEN // PAGE_SIZE
    n_pool = (DEC_BATCH * n_pages * 5) // 4
    f32 = jnp.float32
    nf, nm = N_FOX_LAYERS, N_MLA_LAYERS

    x_prompt = jax.random.normal(ks[0], (BATCH, SEQ, D_MODEL), f32)
    x_sample = jax.random.normal(ks[1], (DEC_BATCH, DEC_SEQ, D_MODEL), f32)

    b_f = jax.random.uniform(ks[2], (nf, FOX_HEADS), f32, minval=1.0, maxval=6.0)

    cache_fox_k = jax.random.normal(ks[3], (nf, n_pool, PAGE_SIZE, FOX_KV_HEADS, FOX_HEAD_DIM), f32)
    cache_fox_v = jax.random.normal(ks[4], (nf, n_pool, PAGE_SIZE, FOX_KV_HEADS, FOX_HEAD_DIM), f32)
    cache_fox_logf = jax.nn.log_sigmoid(
        b_f[:, None, None, :] + jax.random.normal(ks[5], (nf, n_pool, PAGE_SIZE, FOX_HEADS), f32))
    cache_mla_ckv = jax.random.normal(ks[6], (nm, n_pool, PAGE_SIZE, KV_LORA), f32)
    cache_mla_krope = jax.random.normal(ks[7], (nm, n_pool, PAGE_SIZE, MLA_ROPE), f32)
    perm = jax.random.permutation(ks[8], n_pool)
    page_table = perm[:DEC_BATCH * n_pages].reshape(DEC_BATCH, n_pages).astype(jnp.int32)

    def gain(k, shape):
        return 1.0 + 0.01 * jax.random.normal(k, shape, f32)

    norm_fox = gain(ks[9], (nf, D_MODEL))
    w_in_fox = jax.random.normal(ks[10], (nf, D_MODEL, FOX_IN), f32) * D_MODEL ** -0.5
    w_out_fox = jax.random.normal(ks[11], (nf, FOX_WIDTH, D_MODEL), f32) * FOX_WIDTH ** -0.5

    norm_mla = gain(ks[12], (nm, D_MODEL))
    w_in_mla = jax.random.normal(ks[13], (nm, D_MODEL, MLA_IN), f32) * D_MODEL ** -0.5
    q_norm = gain(ks[14], (nm, Q_LORA))
    w_q_b = jax.random.normal(ks[15], (nm, Q_LORA, MLA_HEADS, MLA_QK), f32) * Q_LORA ** -0.5
    kv_norm = gain(ks[16], (nm, KV_LORA))
    w_uk = jax.random.normal(ks[17], (nm, KV_LORA, MLA_HEADS, MLA_NOPE), f32) * KV_LORA ** -0.5
    w_uv = jax.random.normal(ks[18], (nm, KV_LORA, MLA_HEADS, MLA_V), f32) * KV_LORA ** -0.5
    w_out_mla = jax.random.normal(ks[19], (nm, MLA_WIDTH, D_MODEL), f32) * MLA_WIDTH ** -0.5
    final_norm = gain(ks[20], (D_MODEL,))

    return {
        "x_prompt": x_prompt, "x_sample": x_sample,
        "cache_fox_k": cache_fox_k, "cache_fox_v": cache_fox_v, "cache_fox_logf": cache_fox_logf,
        "cache_mla_ckv": cache_mla_ckv, "cache_mla_krope": cache_mla_krope,
        "page_table": page_table,
        "norm_fox": norm_fox, "w_in_fox": w_in_fox, "b_f": b_f, "w_out_fox": w_out_fox,
        "norm_mla": norm_mla, "w_in_mla": w_in_mla, "q_norm": q_norm, "w_q_b": w_q_b,
        "kv_norm": kv_norm, "w_uk": w_uk, "w_uv": w_uv, "w_out_mla": w_out_mla,
        "final_norm": final_norm,
    }


def reference(x_prompt, x_sample, cache_fox_k, cache_fox_v, cache_fox_logf,
              cache_mla_ckv, cache_mla_krope, page_table,
              norm_fox, w_in_fox, b_f, w_out_fox,
              norm_mla, w_in_mla, q_norm, w_q_b, kv_norm, w_uk, w_uv, w_out_mla,
              final_norm):
    hp, hs = x_prompt, x_sample
    fk_p, fv_p, fl_p, fk_s, fv_s, fl_s = [], [], [], [], [], []
    mc_p, mr_p, mc_s, mr_s = [], [], [], []
    for i in range(DEPTH):
        j = i // 2
        if i % 2 == 0:
            yp, k, v, lf = _fox_prompt(_rms_norm(hp, norm_fox[j]), w_in_fox[j], b_f[j], w_out_fox[j])
            fk_p.append(k); fv_p.append(v); fl_p.append(lf.astype(cache_fox_logf.dtype))
            ys, k, v, lf = _fox_sample(_rms_norm(hs, norm_fox[j]), cache_fox_k[j], cache_fox_v[j],
                                       cache_fox_logf[j], page_table, w_in_fox[j], b_f[j], w_out_fox[j])
            fk_s.append(k); fv_s.append(v); fl_s.append(lf.astype(cache_fox_logf.dtype))
        else:
            yp, c, r = _mla_prompt(_rms_norm(hp, norm_mla[j]), w_in_mla[j], q_norm[j], w_q_b[j],
                                   kv_norm[j], w_uk[j], w_uv[j], w_out_mla[j])
            mc_p.append(c); mr_p.append(r)
            ys, c, r = _mla_sample(_rms_norm(hs, norm_mla[j]), cache_mla_ckv[j], cache_mla_krope[j],
                                   page_table, w_in_mla[j], q_norm[j], w_q_b[j], kv_norm[j],
                                   w_uk[j], w_uv[j], w_out_mla[j])
            mc_s.append(c); mr_s.append(r)
        hp = hp + yp
        hs = hs + ys
    y_prompt = _rms_norm(hp, final_norm)
    y_sample = _rms_norm(hs, final_norm)
    return (y_prompt, y_sample,
            jnp.stack(fk_p), jnp.stack(fv_p), jnp.stack(fl_p),
            jnp.stack(fk_s), jnp.stack(fv_s), jnp.stack(fl_s),
            jnp.stack(mc_p), jnp.stack(mr_p),
            jnp.stack(mc_s), jnp.stack(mr_s))
```

```python
import functools

import jax
import jax.numpy as jnp
from jax import lax
from jax.experimental import pallas as pl
from jax.experimental.pallas import tpu as pltpu

F32 = jnp.float32
BF16 = jnp.bfloat16
EPS = 1e-6
ROPE_THETA = 10000.0
LOG2E = 1.4426950408889634
NEG = -1e30
LANES = 128
SUBLANES = 8
VMEM_LIMIT = 52 * 1024 * 1024
PAGES_PER_STEP = 8


def _cparams(*sem):
    return pltpu.CompilerParams(dimension_semantics=sem, vmem_limit_bytes=VMEM_LIMIT)


def _rms(x, g):
    return x * lax.rsqrt(jnp.mean(x * x, axis=-1, keepdims=True) + EPS) * g


def _dot(a, b):
    return jnp.dot(a, b, preferred_element_type=F32)


def _dot_nt(a, b):
    return lax.dot_general(a, b, (((1,), (1,)), ((), ())), preferred_element_type=F32)


def _split3(x):
    hi = x.astype(BF16)
    r1 = x - hi.astype(F32)
    mid = r1.astype(BF16)
    lo = (r1 - mid.astype(F32)).astype(BF16)
    return hi, mid, lo


def _log_sigmoid(x):
    return jnp.minimum(x, 0.0) - jnp.log1p(jnp.exp(-jnp.abs(x)))


def _proj_kernel(*refs, norm, scales):
    if norm:
        x_ref, g_ref, w_ref, o_ref, h_ref = refs
    else:
        x_ref, w_ref, o_ref = refs
    gi = pl.program_id(1)
    if norm:
        @pl.when((gi == 0) & (pl.program_id(2) == 0))
        def _():
            h_ref[...] = _rms(x_ref[...], g_ref[...]).astype(BF16)
        h = h_ref[...]
    else:
        h = x_ref[...]
    acc = _dot(h, w_ref[...])
    if any(s != 1.0 for s in scales):
        sc = jnp.float32(scales[-1])
        for idx in range(len(scales) - 2, -1, -1):
            sc = jnp.where(gi == idx, jnp.float32(scales[idx]), sc)
        acc = acc * sc
    o_ref[...] = acc.astype(o_ref.dtype)


def _proj_stack(x, g, w, scales, *, tm, tn, name):
    m, kd = x.shape
    ng, _, n = w.shape
    norm = g is not None
    tm, tn = min(tm, m), min(tn, n)
    in_specs = [pl.BlockSpec((tm, kd), lambda i, k, j: (i, 0))]
    args = [x]
    scratch = []
    if norm:
        in_specs.append(pl.BlockSpec((1, kd), lambda i, k, j: (0, 0)))
        args.append(g)
        scratch.append(pltpu.VMEM((tm, kd), BF16))
    in_specs.append(pl.BlockSpec((None, kd, tn), lambda i, k, j: (k, 0, j)))
    args.append(w)
    return pl.pallas_call(
        functools.partial(_proj_kernel, norm=norm, scales=tuple(scales)),
        grid=(m // tm, ng, n // tn),
        in_specs=in_specs,
        out_specs=pl.BlockSpec((None, tm, tn), lambda i, k, j: (k, i, j)),
        out_shape=jax.ShapeDtypeStruct((ng, m, n), BF16),
        scratch_shapes=scratch,
        compiler_params=_cparams("parallel", "arbitrary", "arbitrary"),
        name=name,
    )(*args)


def _fox_kvf_kernel(x_ref, g_ref, wk_ref, wv_ref, wf_ref, bf_ref,
                    k_ref, v_ref, lf_ref, kb_ref, vb_ref, *, nh):
    h = _rms(x_ref[...], g_ref[...]).astype(BF16)
    k = _dot(h, wk_ref[...])
    v = _dot(h, wv_ref[...])
    k_ref[...] = k
    v_ref[...] = v
    kb_ref[...] = k.astype(BF16)
    vb_ref[...] = v.astype(BF16)
    lf = _log_sigmoid(_dot(h, wf_ref[...]) + bf_ref[...])
    lane = lax.broadcasted_iota(jnp.int32, lf.shape, 1)
    lf_ref[...] = jnp.where(lane < nh, lf, 0.0)


def _fox_kvf(x, g, wk, wv, wf, bf, *, nh, tm):
    m, kd = x.shape
    kvw = wk.shape[1]
    tm = min(tm, m)
    row = lambda i: (i, 0)
    fix = lambda i: (0, 0)
    return pl.pallas_call(
        functools.partial(_fox_kvf_kernel, nh=nh),
        grid=(m // tm,),
        in_specs=[pl.BlockSpec((tm, kd), row), pl.BlockSpec((1, kd), fix),
                  pl.BlockSpec((kd, kvw), fix), pl.BlockSpec((kd, kvw), fix),
                  pl.BlockSpec((kd, LANES), fix), pl.BlockSpec((1, LANES), fix)],
        out_specs=[pl.BlockSpec((tm, kvw), row), pl.BlockSpec((tm, kvw), row),
                   pl.BlockSpec((tm, LANES), row), pl.BlockSpec((tm, kvw), row),
                   pl.BlockSpec((tm, kvw), row)],
        out_shape=[jax.ShapeDtypeStruct((m, kvw), F32), jax.ShapeDtypeStruct((m, kvw), F32),
                   jax.ShapeDtypeStruct((m, LANES), F32), jax.ShapeDtypeStruct((m, kvw), BF16),
                   jax.ShapeDtypeStruct((m, kvw), BF16)],
        compiler_params=_cparams("parallel"),
        name="fox_kvf",
    )(x, g, wk, wv, wf, bf)


def _fox_aux_kernel(lf_ref, sq_ref, qc_ref, sk_ref, kc_ref, qa_ref, ka_ref, carry_ref):
    @pl.when(pl.program_id(1) == 0)
    def _():
        carry_ref[...] = jnp.zeros_like(carry_ref)
    lf = lf_ref[...]
    tc = lf.shape[0]
    r = lax.broadcasted_iota(jnp.int32, (tc, tc), 0)
    c = lax.broadcasted_iota(jnp.int32, (tc, tc), 1)
    tri = jnp.where(r >= c, 1.0, 0.0).astype(BF16)
    p = _split3(lf)
    cc = carry_ref[...] + (_dot(tri, p[0]) + _dot(tri, p[1]) + _dot(tri, p[2]))
    carry_ref[...] = cc[tc - 1:tc, :]
    p2 = _split3(cc * LOG2E)
    qa = qc_ref[...] + (_dot(p2[0], sq_ref[0]) + _dot(p2[1], sq_ref[1]) + _dot(p2[2], sq_ref[2]))
    ka = kc_ref[...] + (_dot(p2[0], sk_ref[0]) + _dot(p2[1], sk_ref[1]) + _dot(p2[2], sk_ref[2]))
    qa_ref[...] = qa.astype(BF16)
    ka_ref[...] = ka.astype(BF16)


def _fox_aux_tables(nh, nkv):
    grp = nh // nkv
    import numpy as np
    sq = np.zeros((3, LANES, nh * LANES), np.float32)
    qc = np.zeros((1, nh * LANES), np.float32)
    sk = np.zeros((3, LANES, nkv * LANES), np.float32)
    kc = np.zeros((1, nkv * LANES), np.float32)
    for h in range(nh):
        g, j = divmod(h, grp)
        for piece in range(3):
            sq[piece, h, h * LANES + piece] = 1.0
            qc[0, h * LANES + 3 + 3 * j + piece] = 1.0
            sk[piece, h, g * LANES + 3 + 3 * j + piece] = -1.0
    for g in range(nkv):
        kc[0, g * LANES:g * LANES + 3] = 1.0
    return (jnp.asarray(sq, BF16), jnp.asarray(qc, F32), jnp.asarray(sk, BF16), jnp.asarray(kc, F32))


def _fox_aux(lf, tables, *, nh, nkv, tc):
    b, s, _ = lf.shape
    sq, qc, sk, kc = tables
    tc = min(tc, s)
    return pl.pallas_call(
        _fox_aux_kernel,
        grid=(b, s // tc),
        in_specs=[pl.BlockSpec((None, tc, LANES), lambda bi, i: (bi, i, 0)),
                  pl.BlockSpec(sq.shape, lambda bi, i: (0, 0, 0)),
                  pl.BlockSpec(qc.shape, lambda bi, i: (0, 0)),
                  pl.BlockSpec(sk.shape, lambda bi, i: (0, 0, 0)),
                  pl.BlockSpec(kc.shape, lambda bi, i: (0, 0))],
        out_specs=[pl.BlockSpec((None, tc, nh * LANES), lambda bi, i: (bi, i, 0)),
                   pl.BlockSpec((None, tc, nkv * LANES), lambda bi, i: (bi, i, 0))],
        out_shape=[jax.ShapeDtypeStruct((b, s, nh * LANES), BF16),
                   jax.ShapeDtypeStruct((b, s, nkv * LANES), BF16)],
        scratch_shapes=[pltpu.VMEM((1, LANES), F32)],
        compiler_params=_cparams("parallel", "arbitrary"),
        name="fox_aux",
    )(lf, sq, qc, sk, kc)


def _flash_kernel(qm_ref, qa_ref, km_ref, ka_ref, v_ref, o_ref,
                  q_scr, m_scr, l_scr, acc_scr, *, group, tq, tk, hd):
    i = pl.program_id(2)
    rows = group * tq
    for j in range(group):
        q_scr[j * tq:(j + 1) * tq, 0:hd] = qm_ref[:, j * hd:(j + 1) * hd]
        q_scr[j * tq:(j + 1) * tq, hd:2 * hd] = qa_ref[:, j * hd:(j + 1) * hd]
    m_scr[...] = jnp.full_like(m_scr, NEG)
    l_scr[...] = jnp.zeros_like(l_scr)
    acc_scr[...] = jnp.zeros_like(acc_scr)

    def step(ks, mask_off):
        k = jnp.concatenate([km_ref[pl.ds(ks, tk), :], ka_ref[pl.ds(ks, tk), :]], axis=1)
        s = _dot_nt(q_scr[...], k)
        if mask_off is not None:
            t = lax.broadcasted_iota(jnp.int32, (rows, tk), 0) & (tq - 1)
            col = lax.broadcasted_iota(jnp.int32, (rows, tk), 1) + mask_off
            s = jnp.where(col <= t, s, NEG)
        m_prev = m_scr[...]
        m_new = jnp.maximum(m_prev, jnp.max(s, axis=1, keepdims=True))
        alpha = jnp.exp2(m_prev - m_new)
        p = jnp.exp2(s - m_new)
        l_scr[...] = alpha * l_scr[...] + jnp.sum(p, axis=1, keepdims=True)
        acc_scr[...] = alpha * acc_scr[...] + _dot(p.astype(BF16), v_ref[pl.ds(ks, tk), :])
        m_scr[...] = m_new

    per = tq // tk

    def body(jk, carry):
        step(pl.multiple_of(jk * tk, tk), None)
        return carry

    lax.fori_loop(0, i * per, body, 0)
    for d in range(per):
        step(pl.multiple_of(i * tq + d * tk, tk), d * tk)
    o = acc_scr[...] / l_scr[...]
    for j in range(group):
        o_ref[:, j * hd:(j + 1) * hd] = o[j * tq:(j + 1) * tq, :].astype(o_ref.dtype)


def _flash(qm, qa, km, ka, v, *, nkv, group, tq, tk, qm_lead, km_lead, v_lead, ka_shared, name):
    hd = LANES
    b, s = qa.shape[0], qa.shape[1]
    tq, tk = min(tq, s), min(tk, s)

    def spec(lead, blk_rows, width, imap):
        if lead is None:
            return pl.BlockSpec((None, blk_rows, width), imap)
        return pl.BlockSpec((None, None, blk_rows, width), lambda bi, g, i: (lead,) + imap(bi, g, i))

    q_map = lambda bi, g, i: (bi, i, g)
    k_map = lambda bi, g, i: (bi, 0, g)
    ka_map = (lambda bi, g, i: (bi, 0, 0)) if ka_shared else k_map
    return pl.pallas_call(
        functools.partial(_flash_kernel, group=group, tq=tq, tk=tk, hd=hd),
        grid=(b, nkv, s // tq),
        in_specs=[spec(qm_lead, tq, group * hd, q_map), spec(None, tq, group * hd, q_map),
                  spec(km_lead, s, hd, k_map), spec(None, s, hd, ka_map), spec(v_lead, s, hd, k_map)],
        out_specs=pl.BlockSpec((None, tq, group * hd), q_map),
        out_shape=jax.ShapeDtypeStruct((b, s, nkv * group * hd), BF16),
        scratch_shapes=[pltpu.VMEM((group * tq, 2 * hd), BF16), pltpu.VMEM((group * tq, 1), F32),
                        pltpu.VMEM((group * tq, 1), F32), pltpu.VMEM((group * tq, hd), F32)],
        compiler_params=_cparams("parallel", "parallel", "arbitrary"),
        name=name,
    )(qm, qa, km, ka, v)


def _gate(o_ref, z_ref):
    z = z_ref[...].astype(F32)
    return (o_ref[...].astype(F32) * (z * (1.0 / (1.0 + jnp.exp(-z))))).astype(BF16)


def _out_proj_kernel(o_ref, z_ref, w_ref, r_ref, y_ref, u_scr):
    @pl.when(pl.program_id(1) == 0)
    def _():
        u_scr[...] = _gate(o_ref, z_ref)
    y_ref[...] = r_ref[...] + _dot(u_scr[...], w_ref[...])


def _out_proj(o, z, z_lead, w, resid, *, tm, tn):
    m, kd = o.shape
    n = w.shape[1]
    tm, tn = min(tm, m), min(tn, n)
    return pl.pallas_call(
        _out_proj_kernel,
        grid=(m // tm, n // tn),
        in_specs=[pl.BlockSpec((tm, kd), lambda i, j: (i, 0)),
                  pl.BlockSpec((None, tm, kd), lambda i, j: (z_lead, i, 0)),
                  pl.BlockSpec((kd, tn), lambda i, j: (0, j)),
                  pl.BlockSpec((tm, tn), lambda i, j: (i, j))],
        out_specs=pl.BlockSpec((tm, tn), lambda i, j: (i, j)),
        out_shape=jax.ShapeDtypeStruct((m, n), F32),
        scratch_shapes=[pltpu.VMEM((tm, kd), BF16)],
        compiler_params=_cparams("parallel", "arbitrary"),
        name="out_proj",
    )(o, z, w, resid)


def _out_final_kernel(o_ref, z_ref, w_ref, r_ref, g_ref, y_ref):
    hsum = r_ref[...] + _dot(_gate(o_ref, z_ref), w_ref[...])
    y_ref[...] = _rms(hsum, g_ref[...])


def _out_final(o, z, z_lead, w, resid, gfin, *, tm):
    m, kd = o.shape
    n = w.shape[1]
    tm = min(tm, m)
    return pl.pallas_call(
        _out_final_kernel,
        grid=(m // tm,),
        in_specs=[pl.BlockSpec((tm, kd), lambda i: (i, 0)),
                  pl.BlockSpec((None, tm, kd), lambda i: (z_lead, i, 0)),
                  pl.BlockSpec((kd, n), lambda i: (0, 0)),
                  pl.BlockSpec((tm, n), lambda i: (i, 0)),
                  pl.BlockSpec((1, n), lambda i: (0, 0))],
        out_specs=pl.BlockSpec((tm, n), lambda i: (i, 0)),
        out_shape=jax.ShapeDtypeStruct((m, n), F32),
        compiler_params=_cparams("parallel"),
        name="out_final",
    )(o, z, w, resid, gfin)


def _mla_a_kernel(x_ref, g_ref, wqa_ref, wkva_ref, wkr_ref, qn_ref, kvn_ref, cos_ref, sin_ref,
                  qa_ref, ckv_ref, ckvb_ref, kr_ref, krp_ref, *, rope):
    h = _rms(x_ref[...], g_ref[...]).astype(BF16)
    qa_ref[...] = _rms(_dot(h, wqa_ref[...]), qn_ref[...]).astype(BF16)
    ckv = _rms(_dot(h, wkva_ref[...]), kvn_ref[...])
    ckv_ref[...] = ckv
    ckvb_ref[...] = ckv.astype(BF16)
    kr2 = _dot(h, wkr_ref[...])
    krp = kr2[:, :LANES] * cos_ref[...] + kr2[:, LANES:] * sin_ref[...]
    kr_ref[...] = krp[:, :rope]
    krp_ref[...] = krp.astype(BF16)


def _mla_a(x, g, wqa, wkva, wkr2, qn, kvn, cos, sin, *, rope, tm):
    m, kd = x.shape
    ql, kvl = wqa.shape[1], wkva.shape[1]
    tm = min(tm, m)
    nt = cos.shape[0] // tm
    row = lambda i: (i, 0)
    fix = lambda i: (0, 0)
    tab = lambda i: (i % nt, 0)
    return pl.pallas_call(
        functools.partial(_mla_a_kernel, rope=rope),
        grid=(m // tm,),
        in_specs=[pl.BlockSpec((tm, kd), row), pl.BlockSpec((1, kd), fix),
                  pl.BlockSpec((kd, ql), fix), pl.BlockSpec((kd, kvl), fix),
                  pl.BlockSpec((kd, 2 * LANES), fix), pl.BlockSpec((1, ql), fix),
                  pl.BlockSpec((1, kvl), fix), pl.BlockSpec((tm, LANES), tab),
                  pl.BlockSpec((tm, LANES), tab)],
        out_specs=[pl.BlockSpec((tm, ql), row), pl.BlockSpec((tm, kvl), row),
                   pl.BlockSpec((tm, kvl), row), pl.BlockSpec((tm, rope), row),
                   pl.BlockSpec((tm, LANES), row)],
        out_shape=[jax.ShapeDtypeStruct((m, ql), BF16), jax.ShapeDtypeStruct((m, kvl), F32),
                   jax.ShapeDtypeStruct((m, kvl), BF16), jax.ShapeDtypeStruct((m, rope), F32),
                   jax.ShapeDtypeStruct((m, LANES), BF16)],
        compiler_params=_cparams("parallel"),
        name="mla_a",
    )(x, g, wqa, wkva, wkr2, qn, kvn, cos, sin)


def _mla_q_kernel(x_ref, wn_ref, wr_ref, wrr_ref, cos_ref, sin_ref, qm_ref, qa_ref, *, hb, scale):
    x = x_ref[...]
    qm_ref[...] = (_dot(x, wn_ref[...]) * scale).astype(BF16)
    qr = _dot(x, wr_ref[...])
    qrr = _dot(x, wrr_ref[...])
    cos = cos_ref[...]
    sin = sin_ref[...]
    for jj in range(hb):
        sl = slice(jj * LANES, (jj + 1) * LANES)
        qa_ref[:, sl] = ((qr[:, sl] * cos + qrr[:, sl] * sin) * scale).astype(BF16)


def _mla_q(x, wn, wr, wrr, cos, sin, *, scale, tm, hb):
    m, kd = x.shape
    n = wn.shape[1]
    tm = min(tm, m)
    tn = hb * LANES
    nt = cos.shape[0] // tm
    wmap = lambda i, j: (0, j)
    omap = lambda i, j: (i, j)
    tab = lambda i, j: (i % nt, 0)
    return pl.pallas_call(
        functools.partial(_mla_q_kernel, hb=hb, scale=scale),
        grid=(m // tm, n // tn),
        in_specs=[pl.BlockSpec((tm, kd), lambda i, j: (i, 0)),
                  pl.BlockSpec((kd, tn), wmap), pl.BlockSpec((kd, tn), wmap), pl.BlockSpec((kd, tn), wmap),
                  pl.BlockSpec((tm, LANES), tab), pl.BlockSpec((tm, LANES), tab)],
        out_specs=[pl.BlockSpec((tm, tn), omap), pl.BlockSpec((tm, tn), omap)],
        out_shape=[jax.ShapeDtypeStruct((m, n), BF16), jax.ShapeDtypeStruct((m, n), BF16)],
        compiler_params=_cparams("parallel", "arbitrary"),
        name="mla_q",
    )(x, wn, wr, wrr, cos, sin)


def _head_mm_kernel(x_ref, w_ref, o_ref):
    o_ref[...] = _dot(x_ref[...], w_ref[...]).astype(o_ref.dtype)


def _head_mm(x, w, name):
    m = x.shape[0]
    nh, kd, n = w.shape
    return pl.pallas_call(
        _head_mm_kernel,
        grid=(nh,),
        in_specs=[pl.BlockSpec((m, kd), lambda h: (0, h)), pl.BlockSpec((None, kd, n), lambda h: (h, 0, 0))],
        out_specs=pl.BlockSpec((m, n), lambda h: (0, h)),
        out_shape=jax.ShapeDtypeStruct((m, nh * n), BF16),
        compiler_params=_cparams("parallel"),
        name=name,
    )(x, w)


def _softmax_step(s, v, m_scr, l_scr, acc_scr, idx):
    m_prev = m_scr[idx]
    m_new = jnp.maximum(m_prev, jnp.max(s, axis=1, keepdims=True))
    alpha = jnp.exp2(m_prev - m_new)
    p = jnp.exp2(s - m_new)
    l_scr[idx] = alpha * l_scr[idx] + jnp.sum(p, axis=1, keepdims=True)
    acc_scr[idx] = alpha * acc_scr[idx] + _dot(p.astype(BF16), v)
    m_scr[idx] = m_new


def _pad_rows(x, rows):
    return jnp.concatenate([x, jnp.zeros((rows - x.shape[0], x.shape[1]), x.dtype)], axis=0)


def _fox_dec_kernel(tab_ref, q_ref, lfn_ref, kn_ref, vn_ref, *rest, npg, nkv, grp, page):
    kp, vp, lp = rest[:npg], rest[npg:2 * npg], rest[2 * npg:3 * npg]
    o_ref = rest[3 * npg]
    m_scr, l_scr, acc_scr, car_scr, cn_scr = rest[3 * npg + 1:]
    c = pl.program_id(1)
    hd = LANES
    nr = SUBLANES
    rows = grp * nr

    def q_group(g):
        return jnp.concatenate([q_ref[:, (g * grp + j) * hd:(g * grp + j + 1) * hd] for j in range(grp)], axis=0)

    @pl.when(c == 0)
    def _():
        m_scr[...] = jnp.full_like(m_scr, NEG)
        l_scr[...] = jnp.zeros_like(l_scr)
        acc_scr[...] = jnp.zeros_like(acc_scr)
        car_scr[...] = jnp.zeros_like(car_scr)
        lf = lfn_ref[...]
        rid = lax.broadcasted_iota(jnp.int32, lf.shape, 0)
        cn = jnp.zeros_like(lf)
        for t in range(nr):
            cn = cn + jnp.where(rid >= t, lf[t:t + 1, :], 0.0)
        cn = cn * LOG2E
        cn_scr[...] = cn
        cnt = _pad_rows(cn, LANES).T
        t_id = lax.broadcasted_iota(jnp.int32, (nr, LANES), 0)
        s_id = lax.broadcasted_iota(jnp.int32, (nr, LANES), 1)
        for g in range(nkv):
            kg = _pad_rows(kn_ref[:, g * hd:(g + 1) * hd], LANES)
            vg = _pad_rows(vn_ref[:, g * hd:(g + 1) * hd], LANES)
            s = _dot_nt(q_group(g), kg)
            parts = []
            for j in range(grp):
                h = g * grp + j
                sj = s[j * nr:(j + 1) * nr, :] + cn[:, h:h + 1] - cnt[h:h + 1, :]
                parts.append(jnp.where(s_id <= t_id, sj, NEG))
            _softmax_step(jnp.concatenate(parts, axis=0), vg, m_scr, l_scr, acc_scr, g)

    lfs = [lp[p][...] for p in range(npg)]
    pieces = [_split3(x) for x in lfs]
    stacked = jnp.concatenate([pieces[p][k] for k in range(3) for p in range(npg)], axis=0)
    r = lax.broadcasted_iota(jnp.int32, (page, page), 0)
    cc = lax.broadcasted_iota(jnp.int32, (page, page), 1)
    upper = jnp.where(r > cc, 1.0, 0.0).astype(BF16)
    res = _dot(stacked, upper)
    nh = lfs[0].shape[0]
    off = car_scr[...]
    dks = [None] * npg
    for p in range(npg - 1, -1, -1):
        loc = (res[p * nh:(p + 1) * nh] + res[(npg + p) * nh:(npg + p + 1) * nh]
               + res[(2 * npg + p) * nh:(2 * npg + p + 1) * nh])
        dks[p] = (loc + off) * LOG2E
        off = off + (loc[:, 0:1] + lfs[p][:, 0:1])
    car_scr[...] = off
    dk = jnp.concatenate(dks, axis=1)
    cn = cn_scr[...]

    for g in range(nkv):
        kg = jnp.concatenate([kp[p][pl.ds(g, page, stride=nkv), :] for p in range(npg)], axis=0).astype(BF16)
        vg = jnp.concatenate([vp[p][pl.ds(g, page, stride=nkv), :] for p in range(npg)], axis=0).astype(BF16)
        s = _dot_nt(q_group(g), kg)
        parts = []
        for j in range(grp):
            h = g * grp + j
            parts.append(s[j * nr:(j + 1) * nr, :] + cn[:, h:h + 1] + dk[h:h + 1, :])
        _softmax_step(jnp.concatenate(parts, axis=0), vg, m_scr, l_scr, acc_scr, g)

    @pl.when(c == pl.num_programs(1) - 1)
    def _():
        for g in range(nkv):
            o = acc_scr[g] / l_scr[g]
            for j in range(grp):
                h = g * grp + j
                o_ref[:, h * hd:(h + 1) * hd] = o[j * nr:(j + 1) * nr, :].astype(o_ref.dtype)


def _fox_dec(page_table, q, q_lead, lfn, kn, vn, ck, cv, clf, layer, *, nh, nkv):
    bd, n_pages = page_table.shape
    npg = min(PAGES_PER_STEP, n_pages)
    nc = n_pages // npg
    page = clf.shape[-1]
    hd = LANES
    nr = SUBLANES
    grp = nh // nkv
    row = lambda b, c, tab: (b, 0)

    def pmap(p):
        return lambda b, c, tab: (layer, tab[b, (nc - 1 - c) * npg + p], 0, 0)

    in_specs = [pl.BlockSpec((None, nr, nh * hd), lambda b, c, tab: (q_lead, b, 0)),
                pl.BlockSpec((nr, LANES), row), pl.BlockSpec((nr, nkv * hd), row),
                pl.BlockSpec((nr, nkv * hd), row)]
    in_specs += [pl.BlockSpec((None, None, page * nkv, hd), pmap(p)) for p in range(npg)]
    in_specs += [pl.BlockSpec((None, None, page * nkv, hd), pmap(p)) for p in range(npg)]
    in_specs += [pl.BlockSpec((None, None, nh, page), pmap(p)) for p in range(npg)]
    rows = grp * nr
    return pl.pallas_call(
        functools.partial(_fox_dec_kernel, npg=npg, nkv=nkv, grp=grp, page=page),
        grid_spec=pltpu.PrefetchScalarGridSpec(
            num_scalar_prefetch=1, grid=(bd, nc), in_specs=in_specs,
            out_specs=pl.BlockSpec((nr, nh * hd), row),
            scratch_shapes=[pltpu.VMEM((nkv, rows, 1), F32), pltpu.VMEM((nkv, rows, 1), F32),
                            pltpu.VMEM((nkv, rows, hd), F32), pltpu.VMEM((nh, 1), F32),
                            pltpu.VMEM((nr, LANES), F32)]),
        out_shape=jax.ShapeDtypeStruct((bd * nr, nh * hd), BF16),
        compiler_params=_cparams("parallel", "arbitrary"),
        name="fox_dec",
    )(page_table, q, lfn, kn, vn, *([ck] * npg), *([cv] * npg), *([clf] * npg))


def _mla_dec_kernel(tab_ref, ql_ref, qa_ref, cn_ref, krn_ref, *rest, npg, nh, rope, kvl):
    cp, rp = rest[:npg], rest[npg:2 * npg]
    o_ref = rest[2 * npg]
    m_scr, l_scr, acc_scr = rest[2 * npg + 1:]
    c = pl.program_id(1)
    nr = SUBLANES
    ql = jnp.concatenate([ql_ref[:, h * kvl:(h + 1) * kvl] for h in range(nh)], axis=0)
    qa = jnp.concatenate([qa_ref[:, h * LANES:(h + 1) * LANES] for h in range(nh)], axis=0)

    @pl.when(c == 0)
    def _():
        m_scr[...] = jnp.full_like(m_scr, NEG)
        l_scr[...] = jnp.zeros_like(l_scr)
        acc_scr[...] = jnp.zeros_like(acc_scr)
        cnew = _pad_rows(cn_ref[...], LANES)
        s = _dot_nt(ql, cnew) + _dot_nt(qa, _pad_rows(krn_ref[...], LANES))
        t_id = lax.broadcasted_iota(jnp.int32, s.shape, 0) & (nr - 1)
        s_id = lax.broadcasted_iota(jnp.int32, s.shape, 1)
        _softmax_step(jnp.where(s_id <= t_id, s, NEG), cnew, m_scr, l_scr, acc_scr, 0)

    ckv = jnp.concatenate([cp[p][...] for p in range(npg)], axis=0).astype(BF16)
    krt = jnp.concatenate([rp[p][...] for p in range(npg)], axis=1).astype(BF16)
    s = _dot_nt(ql, ckv) + _dot(qa[:, :rope], krt)
    _softmax_step(s, ckv, m_scr, l_scr, acc_scr, 0)

    @pl.when(c == pl.num_programs(1) - 1)
    def _():
        o = acc_scr[0] / l_scr[0]
        for h in range(nh):
            o_ref[:, h * kvl:(h + 1) * kvl] = o[h * nr:(h + 1) * nr, :].astype(o_ref.dtype)


def _mla_dec(page_table, qlat, qaux, cnew, krnew, cckv, ckrt, layer, *, nh):
    bd, n_pages = page_table.shape
    npg = min(PAGES_PER_STEP, n_pages)
    nc = n_pages // npg
    page, kvl = cckv.shape[-2], cckv.shape[-1]
    rope = ckrt.shape[-2]
    nr = SUBLANES
    row = lambda b, c, tab: (b, 0)

    def pmap(p):
        return lambda b, c, tab: (layer, tab[b, c * npg + p], 0, 0)

    in_specs = [pl.BlockSpec((nr, nh * kvl), row), pl.BlockSpec((nr, nh * LANES), row),
                pl.BlockSpec((nr, kvl), row), pl.BlockSpec((nr, LANES), row)]
    in_specs += [pl.BlockSpec((None, None, page, kvl), pmap(p)) for p in range(npg)]
    in_specs += [pl.BlockSpec((None, None, rope, page), pmap(p)) for p in range(npg)]
    return pl.pallas_call(
        functools.partial(_mla_dec_kernel, npg=npg, nh=nh, rope=rope, kvl=kvl),
        grid_spec=pltpu.PrefetchScalarGridSpec(
            num_scalar_prefetch=1, grid=(bd, nc), in_specs=in_specs,
            out_specs=pl.BlockSpec((nr, nh * kvl), row),
            scratch_shapes=[pltpu.VMEM((1, nh * nr, 1), F32), pltpu.VMEM((1, nh * nr, 1), F32),
                            pltpu.VMEM((1, nh * nr, kvl), F32)]),
        out_shape=jax.ShapeDtypeStruct((bd * nr, nh * kvl), BF16),
        compiler_params=_cparams("parallel", "arbitrary"),
        name="mla_dec",
    )(page_table, qlat, qaux, cnew, krnew, *([cckv] * npg), *([ckrt] * npg))


def _rope_tables(pos, rope):
    half = rope // 2
    inv = 1.0 / (ROPE_THETA ** (jnp.arange(half, dtype=F32) * (2.0 / rope)))
    ang = pos.astype(F32)[:, None] * inv[None, :]
    zeros = jnp.zeros((pos.shape[0], LANES - rope), F32)
    cos = jnp.concatenate([jnp.cos(ang), jnp.cos(ang), zeros], axis=1)
    sin = jnp.concatenate([jnp.sin(ang), jnp.sin(ang), zeros], axis=1)
    return cos, sin


def _rot_cols(w):
    half = w.shape[-1] // 2
    return jnp.concatenate([-w[..., half:], w[..., :half]], axis=-1)


def _pad_last(w, width):
    pad = [(0, 0)] * (w.ndim - 1) + [(0, width - w.shape[-1])]
    return jnp.pad(w, pad)


def kernel(x_prompt, x_sample, cache_fox_k, cache_fox_v, cache_fox_logf, cache_mla_ckv, cache_mla_krope, page_table, norm_fox, w_in_fox, b_f, w_out_fox, norm_mla, w_in_mla, q_norm, w_q_b, kv_norm, w_uk, w_uv, w_out_mla, final_norm):
    bsz, seq, dm = x_prompt.shape
    bd, ns, _ = x_sample.shape
    nf, n_pool, page, nkv, hd = cache_fox_k.shape
    nm = cache_mla_ckv.shape[0]
    nh = b_f.shape[1]
    grp = nh // nkv
    fw, kvw = nh * hd, nkv * hd
    n_pages = page_table.shape[1]
    past = n_pages * page
    _, ql, mh, qk = w_q_b.shape
    kvl, nope = w_uk.shape[1], w_uk.shape[3]
    rope = qk - nope
    vd = w_uv.shape[3]
    assert hd == LANES and nope == LANES and vd == LANES and rope <= LANES and ns <= SUBLANES
    nr = SUBLANES
    mp, ms = bsz * seq, bd * nr

    hp = x_prompt.reshape(mp, dm)
    hs = jnp.pad(x_sample, ((0, 0), (0, nr - ns), (0, 0))).reshape(ms, dm)

    ck = cache_fox_k.reshape(nf, n_pool, page * nkv, hd)
    cv = cache_fox_v.reshape(nf, n_pool, page * nkv, hd)
    clf = jnp.swapaxes(cache_fox_logf, 2, 3)
    ckrt = jnp.swapaxes(cache_mla_krope, 2, 3)

    cos_p, sin_p = _rope_tables(jnp.arange(seq), rope)
    cos_s, sin_s = _rope_tables(past + jnp.arange(nr), rope)
    cos_s, sin_s = jnp.tile(cos_s, (bd, 1)), jnp.tile(sin_s, (bd, 1))
    aux_tables = _fox_aux_tables(nh, nkv)
    fox_scale = hd ** -0.5 * LOG2E
    mla_scale = qk ** -0.5 * LOG2E
    gfin = final_norm.reshape(1, dm)

    outs = {k: [] for k in ("fk_p", "fv_p", "fl_p", "fk_s", "fv_s", "fl_s", "mc_p", "mr_p", "mc_s", "mr_s")}
    depth = nf + nm
    yp = ys = None
    for i in range(depth):
        j = i // 2
        last = i == depth - 1
        if i % 2 == 0:
            w = w_in_fox[j]
            o1, o2, o3, o4 = fw, fw + kvw, fw + 2 * kvw, fw + 2 * kvw + nh
            w_qz = jnp.stack([w[:, :o1], w[:, o4:]]).astype(BF16)
            wk, wv = w[:, o1:o2].astype(BF16), w[:, o2:o3].astype(BF16)
            wf = _pad_last(w[:, o3:o4], LANES).astype(BF16)
            bfp = _pad_last(b_f[j].reshape(1, nh), LANES)
            g = norm_fox[j].reshape(1, dm)
            w_out = w_out_fox[j].astype(BF16)

            qz = _proj_stack(hp, g, w_qz, (fox_scale, 1.0), tm=1024, tn=512, name="fox_qz")
            k, v, lf, kb, vb = _fox_kvf(hp, g, wk, wv, wf, bfp, nh=nh, tm=512)
            qa, ka = _fox_aux(lf.reshape(bsz, seq, LANES), aux_tables, nh=nh, nkv=nkv, tc=512)
            o = _flash(qz.reshape(2, bsz, seq, fw), qa, kb.reshape(bsz, seq, kvw), ka,
                       vb.reshape(bsz, seq, kvw), nkv=nkv, group=grp, tq=256, tk=256,
                       qm_lead=0, km_lead=None, v_lead=None, ka_shared=False, name="fox_flash")
            o = o.reshape(mp, fw)
            outs["fk_p"].append(k.reshape(bsz, seq, nkv, hd))
            outs["fv_p"].append(v.reshape(bsz, seq, nkv, hd))
            outs["fl_p"].append(lf[:, :nh].reshape(bsz, seq, nh))
            if last:
                yp = _out_final(o, qz, 1, w_out, hp, gfin, tm=256)
            else:
                hp = _out_proj(o, qz, 1, w_out, hp, tm=512, tn=512)

            qz = _proj_stack(hs, g, w_qz, (fox_scale, 1.0), tm=1024, tn=512, name="fox_qz_s")
            k, v, lf, kb, vb = _fox_kvf(hs, g, wk, wv, wf, bfp, nh=nh, tm=512)
            o = _fox_dec(page_table, qz, 0, lf, kb, vb, ck, cv, clf, j, nh=nh, nkv=nkv)
            outs["fk_s"].append(k.reshape(bd, nr, nkv, hd)[:, :ns])
            outs["fv_s"].append(v.reshape(bd, nr, nkv, hd)[:, :ns])
            outs["fl_s"].append(lf[:, :nh].reshape(bd, nr, nh)[:, :ns])
            if last:
                ys = _out_final(o, qz, 1, w_out, hs, gfin, tm=256)
            else:
                hs = _out_proj(o, qz, 1, w_out, hs, tm=512, tn=512)
        else:
            w = w_in_mla[j]
            o1, o2, o3 = ql, ql + kvl, ql + kvl + rope
            wqa, wkva = w[:, :o1].astype(BF16), w[:, o1:o2].astype(BF16)
            wkr = w[:, o2:o3]
            wkr2 = jnp.concatenate([_pad_last(wkr, LANES), _pad_last(_rot_cols(wkr), LANES)], axis=1).astype(BF16)
            wz = w[:, o3:].astype(BF16)[None]
            g = norm_mla[j].reshape(1, dm)
            qn, kvn = q_norm[j].reshape(1, ql), kv_norm[j].reshape(1, kvl)
            wqb = w_q_b[j]
            wn = wqb[:, :, :nope].reshape(ql, mh * nope).astype(BF16)
            wr = _pad_last(wqb[:, :, nope:], LANES).reshape(ql, mh * LANES).astype(BF16)
            wrr = _pad_last(_rot_cols(wqb[:, :, nope:]), LANES).reshape(ql, mh * LANES).astype(BF16)
            w_kv = jnp.stack([w_uk[j].reshape(kvl, mh * nope), w_uv[j].reshape(kvl, mh * vd)]).astype(BF16)
            w_ukt = jnp.transpose(w_uk[j], (1, 2, 0)).astype(BF16)
            w_uvh = jnp.transpose(w_uv[j], (1, 0, 2)).astype(BF16)
            w_out = w_out_mla[j].astype(BF16)

            z = _proj_stack(hp, g, wz, (1.0,), tm=1024, tn=512, name="mla_z")
            qan, ckv, ckvb, kr, krp = _mla_a(hp, g, wqa, wkva, wkr2, qn, kvn, cos_p, sin_p, rope=rope, tm=512)
            qm, qx = _mla_q(qan, wn, wr, wrr, cos_p, sin_p, scale=mla_scale, tm=1024, hb=4)
            kvu = _proj_stack(ckvb, None, w_kv, (1.0, 1.0), tm=1024, tn=1024, name="mla_kv")
            o = _flash(qm.reshape(bsz, seq, mh * nope), qx.reshape(bsz, seq, mh * LANES),
                       kvu.reshape(2, bsz, seq, mh * nope), krp.reshape(bsz, seq, LANES),
                       kvu.reshape(2, bsz, seq, mh * vd), nkv=mh, group=1, tq=256, tk=256,
                       qm_lead=None, km_lead=0, v_lead=1, ka_shared=True, name="mla_flash")
            o = o.reshape(mp, mh * vd)
            outs["mc_p"].append(ckv.reshape(bsz, seq, kvl))
            outs["mr_p"].append(kr.reshape(bsz, seq, rope))
            if last:
                yp = _out_final(o, z, 0, w_out, hp, gfin, tm=256)
            else:
                hp = _out_proj(o, z, 0, w_out, hp, tm=512, tn=512)

            z = _proj_stack(hs, g, wz, (1.0,), tm=1024, tn=512, name="mla_z_s")
            qan, ckv, ckvb, kr, krp = _mla_a(hs, g, wqa, wkva, wkr2, qn, kvn, cos_s, sin_s, rope=rope, tm=512)
            qm, qx = _mla_q(qan, wn, wr, wrr, cos_s, sin_s, scale=mla_scale, tm=1024, hb=4)
            qlat = _head_mm(qm, w_ukt, "mla_absorb")
            olat = _mla_dec(page_table, qlat, qx, ckvb, krp, cache_mla_ckv, ckrt, j, nh=mh)
            o = _head_mm(olat, w_uvh, "mla_uv")
            outs["mc_s"].append(ckv.reshape(bd, nr, kvl)[:, :ns])
            outs["mr_s"].append(kr.reshape(bd, nr, rope)[:, :ns])
            if last:
                ys = _out_final(o, z, 0, w_out, hs, gfin, tm=256)
            else:
                hs = _out_proj(o, z, 0, w_out, hs, tm=512, tn=512)

    y_prompt = yp.reshape(bsz, seq, dm)
    y_sample = ys.reshape(bd, nr, dm)[:, :ns]
    st = lambda key: jnp.stack(outs[key])
    return (y_prompt, y_sample, st("fk_p"), st("fv_p"), st("fl_p"), st("fk_s"), st("fv_s"), st("fl_s"),
            st("mc_p"), st("mr_p"), st("mc_s"), st("mr_s"))
```

```python
import functools

import jax
import jax.numpy as jnp
from jax import lax
from jax.experimental import pallas as pl
from jax.experimental.pallas import tpu as pltpu

F32 = jnp.float32
BF16 = jnp.bfloat16
EPS = 1e-6
ROPE_THETA = 10000.0
LOG2E = 1.4426950408889634
NEG = -1e30
LANES = 128
SUBLANES = 8
VMEM_LIMIT = 52 * 1024 * 1024
PAGES_PER_STEP = 8


def _cparams(*sem):
    return pltpu.CompilerParams(dimension_semantics=sem, vmem_limit_bytes=VMEM_LIMIT)


def _rms(x, g):
    return x * lax.rsqrt(jnp.mean(x * x, axis=-1, keepdims=True) + EPS) * g


def _dot(a, b):
    return jnp.dot(a, b, preferred_element_type=F32)


def _dot_nt(a, b):
    return lax.dot_general(a, b, (((1,), (1,)), ((), ())), preferred_element_type=F32)


def _split3(x):
    hi = x.astype(BF16)
    r1 = x - hi.astype(F32)
    mid = r1.astype(BF16)
    lo = (r1 - mid.astype(F32)).astype(BF16)
    return hi, mid, lo


def _log_sigmoid(x):
    return jnp.minimum(x, 0.0) - jnp.log1p(jnp.exp(-jnp.abs(x)))


def _proj_kernel(*refs, norm, scales):
    if norm:
        x_ref, g_ref, w_ref, o_ref, h_ref = refs
    else:
        x_ref, w_ref, o_ref = refs
    gi = pl.program_id(1)
    if norm:
        @pl.when((gi == 0) & (pl.program_id(2) == 0))
        def _():
            h_ref[...] = _rms(x_ref[...], g_ref[...]).astype(BF16)
        h = h_ref[...]
    else:
        h = x_ref[...]
    acc = _dot(h, w_ref[...])
    if any(s != 1.0 for s in scales):
        sc = jnp.float32(scales[-1])
        for idx in range(len(scales) - 2, -1, -1):
            sc = jnp.where(gi == idx, jnp.float32(scales[idx]), sc)
        acc = acc * sc
    o_ref[...] = acc.astype(o_ref.dtype)


def _proj_stack(x, g, w, scales, *, tm, tn, name):
    m, kd = x.shape
    ng, _, n = w.shape
    norm = g is not None
    tm, tn = min(tm, m), min(tn, n)
    in_specs = [pl.BlockSpec((tm, kd), lambda i, k, j: (i, 0))]
    args = [x]
    scratch = []
    if norm:
        in_specs.append(pl.BlockSpec((1, kd), lambda i, k, j: (0, 0)))
        args.append(g)
        scratch.append(pltpu.VMEM((tm, kd), BF16))
    in_specs.append(pl.BlockSpec((None, kd, tn), lambda i, k, j: (k, 0, j)))
    args.append(w)
    return pl.pallas_call(
        functools.partial(_proj_kernel, norm=norm, scales=tuple(scales)),
        grid=(m // tm, ng, n // tn),
        in_specs=in_specs,
        out_specs=pl.BlockSpec((None, tm, tn), lambda i, k, j: (k, i, j)),
        out_shape=jax.ShapeDtypeStruct((ng, m, n), BF16),
        scratch_shapes=scratch,
        compiler_params=_cparams("parallel", "arbitrary", "arbitrary"),
        name=name,
    )(*args)


def _fox_kvf_kernel(x_ref, g_ref, wk_ref, wv_ref, wf_ref, bf_ref,
                    k_ref, v_ref, lf_ref, kb_ref, vb_ref, *, nh):
    h = _rms(x_ref[...], g_ref[...]).astype(BF16)
    k = _dot(h, wk_ref[...])
    v = _dot(h, wv_ref[...])
    k_ref[...] = k
    v_ref[...] = v
    kb_ref[...] = k.astype(BF16)
    vb_ref[...] = v.astype(BF16)
    lf = _log_sigmoid(_dot(h, wf_ref[...]) + bf_ref[...])
    lane = lax.broadcasted_iota(jnp.int32, lf.shape, 1)
    lf_ref[...] = jnp.where(lane < nh, lf, 0.0)


def _fox_kvf(x, g, wk, wv, wf, bf, *, nh, tm):
    m, kd = x.shape
    kvw = wk.shape[1]
    tm = min(tm, m)
    row = lambda i: (i, 0)
    fix = lambda i: (0, 0)
    return pl.pallas_call(
        functools.partial(_fox_kvf_kernel, nh=nh),
        grid=(m // tm,),
        in_specs=[pl.BlockSpec((tm, kd), row), pl.BlockSpec((1, kd), fix),
                  pl.BlockSpec((kd, kvw), fix), pl.BlockSpec((kd, kvw), fix),
                  pl.BlockSpec((kd, LANES), fix), pl.BlockSpec((1, LANES), fix)],
        out_specs=[pl.BlockSpec((tm, kvw), row), pl.BlockSpec((tm, kvw), row),
                   pl.BlockSpec((tm, LANES), row), pl.BlockSpec((tm, kvw), row),
                   pl.BlockSpec((tm, kvw), row)],
        out_shape=[jax.ShapeDtypeStruct((m, kvw), F32), jax.ShapeDtypeStruct((m, kvw), F32),
                   jax.ShapeDtypeStruct((m, LANES), F32), jax.ShapeDtypeStruct((m, kvw), BF16),
                   jax.ShapeDtypeStruct((m, kvw), BF16)],
        compiler_params=_cparams("parallel"),
        name="fox_kvf",
    )(x, g, wk, wv, wf, bf)


def _fox_aux_kernel(lf_ref, sq_ref, qc_ref, sk_ref, kc_ref, qa_ref, ka_ref, carry_ref):
    @pl.when(pl.program_id(1) == 0)
    def _():
        carry_ref[...] = jnp.zeros_like(carry_ref)
    lf = lf_ref[...]
    tc = lf.shape[0]
    r = lax.broadcasted_iota(jnp.int32, (tc, tc), 0)
    c = lax.broadcasted_iota(jnp.int32, (tc, tc), 1)
    tri = jnp.where(r >= c, 1.0, 0.0).astype(BF16)
    p = _split3(lf)
    cc = carry_ref[...] + (_dot(tri, p[0]) + _dot(tri, p[1]) + _dot(tri, p[2]))
    carry_ref[...] = cc[tc - 1:tc, :]
    p2 = _split3(cc * LOG2E)
    qa = qc_ref[...] + (_dot(p2[0], sq_ref[0]) + _dot(p2[1], sq_ref[1]) + _dot(p2[2], sq_ref[2]))
    ka = kc_ref[...] + (_dot(p2[0], sk_ref[0]) + _dot(p2[1], sk_ref[1]) + _dot(p2[2], sk_ref[2]))
    qa_ref[...] = qa.astype(BF16)
    ka_ref[...] = ka.astype(BF16)


def _fox_aux_tables(nh, nkv):
    grp = nh // nkv
    import numpy as np
    sq = np.zeros((3, LANES, nh * LANES), np.float32)
    qc = np.zeros((1, nh * LANES), np.float32)
    sk = np.zeros((3, LANES, nkv * LANES), np.float32)
    kc = np.zeros((1, nkv * LANES), np.float32)
    for h in range(nh):
        g, j = divmod(h, grp)
        for piece in range(3):
            sq[piece, h, h * LANES + piece] = 1.0
            qc[0, h * LANES + 3 + 3 * j + piece] = 1.0
            sk[piece, h, g * LANES + 3 + 3 * j + piece] = -1.0
    for g in range(nkv):
        kc[0, g * LANES:g * LANES + 3] = 1.0
    return (jnp.asarray(sq, BF16), jnp.asarray(qc, F32), jnp.asarray(sk, BF16), jnp.asarray(kc, F32))


def _fox_aux(lf, tables, *, nh, nkv, tc):
    b, s, _ = lf.shape
    sq, qc, sk, kc = tables
    tc = min(tc, s)
    return pl.pallas_call(
        _fox_aux_kernel,
        grid=(b, s // tc),
        in_specs=[pl.BlockSpec((None, tc, LANES), lambda bi, i: (bi, i, 0)),
                  pl.BlockSpec(sq.shape, lambda bi, i: (0, 0, 0)),
                  pl.BlockSpec(qc.shape, lambda bi, i: (0, 0)),
                  pl.BlockSpec(sk.shape, lambda bi, i: (0, 0, 0)),
                  pl.BlockSpec(kc.shape, lambda bi, i: (0, 0))],
        out_specs=[pl.BlockSpec((None, tc, nh * LANES), lambda bi, i: (bi, i, 0)),
                   pl.BlockSpec((None, tc, nkv * LANES), lambda bi, i: (bi, i, 0))],
        out_shape=[jax.ShapeDtypeStruct((b, s, nh * LANES), BF16),
                   jax.ShapeDtypeStruct((b, s, nkv * LANES), BF16)],
        scratch_shapes=[pltpu.VMEM((1, LANES), F32)],
        compiler_params=_cparams("parallel", "arbitrary"),
        name="fox_aux",
    )(lf, sq, qc, sk, kc)


def _flash_kernel(qm_ref, qa_ref, km_ref, ka_ref, v_ref, o_ref,
                  q_scr, s0_scr, s1_scr, m_scr, acc_scr, *, group, tq, tk, hd):
    i = pl.program_id(2)
    rows = group * tq
    per = tq // tk
    s_bufs = (s0_scr, s1_scr)
    for j in range(group):
        q_scr[j * tq:(j + 1) * tq, 0:hd] = qm_ref[:, j * hd:(j + 1) * hd]
        q_scr[j * tq:(j + 1) * tq, hd:2 * hd] = qa_ref[:, j * hd:(j + 1) * hd]
    m_scr[...] = jnp.full_like(m_scr, NEG)
    acc_scr[...] = jnp.zeros_like(acc_scr)
    ones = jnp.ones((tk, hd), BF16)

    def scores(step, slot, mask_off):
        ks = pl.multiple_of(step * tk, tk)
        k = jnp.concatenate([km_ref[pl.ds(ks, tk), :], ka_ref[pl.ds(ks, tk), :]], axis=1)
        s = _dot_nt(q_scr[...], k)
        if mask_off is not None:
            t = lax.broadcasted_iota(jnp.int32, (rows, tk), 0) & (tq - 1)
            col = lax.broadcasted_iota(jnp.int32, (rows, tk), 1) + mask_off
            s = jnp.where(col <= t, s, NEG)
        s_bufs[slot][...] = s

    def consume(step, slot):
        ks = pl.multiple_of(step * tk, tk)
        s_ref = s_bufs[slot]
        m_prev = m_scr[...]
        m_new = jnp.maximum(m_prev, jnp.max(s_ref[...], axis=1, keepdims=True))
        alpha = jnp.exp2(m_prev - m_new)
        p = jnp.exp2(s_ref[...] - jnp.tile(m_new, (1, tk // hd)))
        v1 = jnp.concatenate([v_ref[pl.ds(ks, tk), :], ones], axis=1)
        acc_scr[...] = jnp.tile(alpha, (1, 2)) * acc_scr[...] + _dot(p.astype(BF16), v1)
        m_scr[...] = m_new

    def tail(first, n_plain):
        seq = [(first + n, None) for n in range(n_plain)]
        seq += [(first + n_plain + d, d * tk) for d in range(per)]
        for n, (step, _) in enumerate(seq):
            if n + 1 < len(seq):
                scores(seq[n + 1][0], (n + 1) % 2, seq[n + 1][1])
            consume(step, n % 2)

    n_plain = i * per

    @pl.when(i == 0)
    def _():
        scores(0, 0, 0)
        tail(0, 0)

    @pl.when(i > 0)
    def _():
        scores(0, 0, None)
        n_pairs = (n_plain - 1) // 2

        def body(jj, carry):
            scores(2 * jj + 1, 1, None)
            consume(2 * jj, 0)
            scores(2 * jj + 2, 0, None)
            consume(2 * jj + 1, 1)
            return carry

        lax.fori_loop(0, n_pairs, body, 0)
        if per % 2 == 0:
            tail(2 * n_pairs, 2)
        else:
            @pl.when(n_plain % 2 == 1)
            def _():
                tail(2 * n_pairs, 1)

            @pl.when(n_plain % 2 == 0)
            def _():
                tail(2 * n_pairs, 2)

    acc = acc_scr[...]
    o = acc[:, :hd] / acc[:, hd:]
    for j in range(group):
        o_ref[:, j * hd:(j + 1) * hd] = o[j * tq:(j + 1) * tq, :].astype(o_ref.dtype)


def _flash(qm, qa, km, ka, v, *, nkv, group, tq, tk, qm_lead, km_lead, v_lead, ka_shared, name):
    hd = LANES
    b, s = qa.shape[0], qa.shape[1]
    tq, tk = min(tq, s), min(tk, s)

    def spec(lead, blk_rows, width, imap):
        if lead is None:
            return pl.BlockSpec((None, blk_rows, width), imap)
        return pl.BlockSpec((None, None, blk_rows, width), lambda bi, g, i: (lead,) + imap(bi, g, i))

    q_map = lambda bi, g, i: (bi, i, g)
    k_map = lambda bi, g, i: (bi, 0, g)
    ka_map = (lambda bi, g, i: (bi, 0, 0)) if ka_shared else k_map
    return pl.pallas_call(
        functools.partial(_flash_kernel, group=group, tq=tq, tk=tk, hd=hd),
        grid=(b, nkv, s // tq),
        in_specs=[spec(qm_lead, tq, group * hd, q_map), spec(None, tq, group * hd, q_map),
                  spec(km_lead, s, hd, k_map), spec(None, s, hd, ka_map), spec(v_lead, s, hd, k_map)],
        out_specs=pl.BlockSpec((None, tq, group * hd), q_map),
        out_shape=jax.ShapeDtypeStruct((b, s, nkv * group * hd), BF16),
        scratch_shapes=[pltpu.VMEM((group * tq, 2 * hd), BF16), pltpu.VMEM((group * tq, tk), F32),
                        pltpu.VMEM((group * tq, tk), F32), pltpu.VMEM((group * tq, hd), F32),
                        pltpu.VMEM((group * tq, 2 * hd), F32)],
        compiler_params=_cparams("parallel", "parallel", "arbitrary"),
        name=name,
    )(qm, qa, km, ka, v)


def _gate(o_ref, z_ref):
    z = z_ref[...].astype(F32)
    return (o_ref[...].astype(F32) * (z * (1.0 / (1.0 + jnp.exp(-z))))).astype(BF16)


def _out_proj_kernel(o_ref, z_ref, w_ref, r_ref, y_ref, u_scr):
    @pl.when(pl.program_id(1) == 0)
    def _():
        u_scr[...] = _gate(o_ref, z_ref)
    y_ref[...] = r_ref[...] + _dot(u_scr[...], w_ref[...])


def _out_proj(o, z, z_lead, w, resid, *, tm, tn):
    m, kd = o.shape
    n = w.shape[1]
    tm, tn = min(tm, m), min(tn, n)
    return pl.pallas_call(
        _out_proj_kernel,
        grid=(m // tm, n // tn),
        in_specs=[pl.BlockSpec((tm, kd), lambda i, j: (i, 0)),
                  pl.BlockSpec((None, tm, kd), lambda i, j: (z_lead, i, 0)),
                  pl.BlockSpec((kd, tn), lambda i, j: (0, j)),
                  pl.BlockSpec((tm, tn), lambda i, j: (i, j))],
        out_specs=pl.BlockSpec((tm, tn), lambda i, j: (i, j)),
        out_shape=jax.ShapeDtypeStruct((m, n), F32),
        scratch_shapes=[pltpu.VMEM((tm, kd), BF16)],
        compiler_params=_cparams("parallel", "arbitrary"),
        name="out_proj",
    )(o, z, w, resid)


def _out_final_kernel(o_ref, z_ref, w_ref, r_ref, g_ref, y_ref):
    hsum = r_ref[...] + _dot(_gate(o_ref, z_ref), w_ref[...])
    y_ref[...] = _rms(hsum, g_ref[...])


def _out_final(o, z, z_lead, w, resid, gfin, *, tm):
    m, kd = o.shape
    n = w.shape[1]
    tm = min(tm, m)
    return pl.pallas_call(
        _out_final_kernel,
        grid=(m // tm,),
        in_specs=[pl.BlockSpec((tm, kd), lambda i: (i, 0)),
                  pl.BlockSpec((None, tm, kd), lambda i: (z_lead, i, 0)),
                  pl.BlockSpec((kd, n), lambda i: (0, 0)),
                  pl.BlockSpec((tm, n), lambda i: (i, 0)),
                  pl.BlockSpec((1, n), lambda i: (0, 0))],
        out_specs=pl.BlockSpec((tm, n), lambda i: (i, 0)),
        out_shape=jax.ShapeDtypeStruct((m, n), F32),
        compiler_params=_cparams("parallel"),
        name="out_final",
    )(o, z, w, resid, gfin)


def _mla_a_kernel(x_ref, g_ref, wqa_ref, wkva_ref, wkr_ref, qn_ref, kvn_ref, cos_ref, sin_ref,
                  qa_ref, ckv_ref, ckvb_ref, kr_ref, krp_ref, *, rope):
    h = _rms(x_ref[...], g_ref[...]).astype(BF16)
    qa_ref[...] = _rms(_dot(h, wqa_ref[...]), qn_ref[...]).astype(BF16)
    ckv = _rms(_dot(h, wkva_ref[...]), kvn_ref[...])
    ckv_ref[...] = ckv
    ckvb_ref[...] = ckv.astype(BF16)
    kr2 = _dot(h, wkr_ref[...])
    krp = kr2[:, :LANES] * cos_ref[...] + kr2[:, LANES:] * sin_ref[...]
    kr_ref[...] = krp[:, :rope]
    krp_ref[...] = krp.astype(BF16)


def _mla_a(x, g, wqa, wkva, wkr2, qn, kvn, cos, sin, *, rope, tm):
    m, kd = x.shape
    ql, kvl = wqa.shape[1], wkva.shape[1]
    tm = min(tm, m)
    nt = cos.shape[0] // tm
    row = lambda i: (i, 0)
    fix = lambda i: (0, 0)
    tab = lambda i: (i % nt, 0)
    return pl.pallas_call(
        functools.partial(_mla_a_kernel, rope=rope),
        grid=(m // tm,),
        in_specs=[pl.BlockSpec((tm, kd), row), pl.BlockSpec((1, kd), fix),
                  pl.BlockSpec((kd, ql), fix), pl.BlockSpec((kd, kvl), fix),
                  pl.BlockSpec((kd, 2 * LANES), fix), pl.BlockSpec((1, ql), fix),
                  pl.BlockSpec((1, kvl), fix), pl.BlockSpec((tm, LANES), tab),
                  pl.BlockSpec((tm, LANES), tab)],
        out_specs=[pl.BlockSpec((tm, ql), row), pl.BlockSpec((tm, kvl), row),
                   pl.BlockSpec((tm, kvl), row), pl.BlockSpec((tm, rope), row),
                   pl.BlockSpec((tm, LANES), row)],
        out_shape=[jax.ShapeDtypeStruct((m, ql), BF16), jax.ShapeDtypeStruct((m, kvl), F32),
                   jax.ShapeDtypeStruct((m, kvl), BF16), jax.ShapeDtypeStruct((m, rope), F32),
                   jax.ShapeDtypeStruct((m, LANES), BF16)],
        compiler_params=_cparams("parallel"),
        name="mla_a",
    )(x, g, wqa, wkva, wkr2, qn, kvn, cos, sin)


def _mla_q_kernel(x_ref, wn_ref, wr_ref, wrr_ref, cos_ref, sin_ref, qm_ref, qa_ref, *, hb, scale):
    x = x_ref[...]
    qm_ref[...] = (_dot(x, wn_ref[...]) * scale).astype(BF16)
    qr = _dot(x, wr_ref[...])
    qrr = _dot(x, wrr_ref[...])
    cos = cos_ref[...]
    sin = sin_ref[...]
    for jj in range(hb):
        sl = slice(jj * LANES, (jj + 1) * LANES)
        qa_ref[:, sl] = ((qr[:, sl] * cos + qrr[:, sl] * sin) * scale).astype(BF16)


def _mla_q(x, wn, wr, wrr, cos, sin, *, scale, tm, hb):
    m, kd = x.shape
    n = wn.shape[1]
    tm = min(tm, m)
    tn = hb * LANES
    nt = cos.shape[0] // tm
    wmap = lambda i, j: (0, j)
    omap = lambda i, j: (i, j)
    tab = lambda i, j: (i % nt, 0)
    return pl.pallas_call(
        functools.partial(_mla_q_kernel, hb=hb, scale=scale),
        grid=(m // tm, n // tn),
        in_specs=[pl.BlockSpec((tm, kd), lambda i, j: (i, 0)),
                  pl.BlockSpec((kd, tn), wmap), pl.BlockSpec((kd, tn), wmap), pl.BlockSpec((kd, tn), wmap),
                  pl.BlockSpec((tm, LANES), tab), pl.BlockSpec((tm, LANES), tab)],
        out_specs=[pl.BlockSpec((tm, tn), omap), pl.BlockSpec((tm, tn), omap)],
        out_shape=[jax.ShapeDtypeStruct((m, n), BF16), jax.ShapeDtypeStruct((m, n), BF16)],
        compiler_params=_cparams("parallel", "arbitrary"),
        name="mla_q",
    )(x, wn, wr, wrr, cos, sin)


def _head_mm_kernel(x_ref, w_ref, o_ref):
    o_ref[...] = _dot(x_ref[...], w_ref[...]).astype(o_ref.dtype)


def _head_mm(x, w, name):
    m = x.shape[0]
    nh, kd, n = w.shape
    return pl.pallas_call(
        _head_mm_kernel,
        grid=(nh,),
        in_specs=[pl.BlockSpec((m, kd), lambda h: (0, h)), pl.BlockSpec((None, kd, n), lambda h: (h, 0, 0))],
        out_specs=pl.BlockSpec((m, n), lambda h: (0, h)),
        out_shape=jax.ShapeDtypeStruct((m, nh * n), BF16),
        compiler_params=_cparams("parallel"),
        name=name,
    )(x, w)


def _softmax_step(s, v, m_scr, l_scr, acc_scr, idx):
    m_prev = m_scr[idx]
    m_new = jnp.maximum(m_prev, jnp.max(s, axis=1, keepdims=True))
    alpha = jnp.exp2(m_prev - m_new)
    p = jnp.exp2(s - m_new)
    l_scr[idx] = alpha * l_scr[idx] + jnp.sum(p, axis=1, keepdims=True)
    acc_scr[idx] = alpha * acc_scr[idx] + _dot(p.astype(BF16), v)
    m_scr[idx] = m_new


def _pad_rows(x, rows):
    return jnp.concatenate([x, jnp.zeros((rows - x.shape[0], x.shape[1]), x.dtype)], axis=0)


def _fox_dec_kernel(tab_ref, q_ref, lfn_ref, kn_ref, vn_ref, *rest, npg, nkv, grp, page):
    kp, vp, lp = rest[:npg], rest[npg:2 * npg], rest[2 * npg:3 * npg]
    o_ref = rest[3 * npg]
    m_scr, l_scr, acc_scr, car_scr, cn_scr = rest[3 * npg + 1:]
    c = pl.program_id(1)
    hd = LANES
    nr = SUBLANES
    grows = grp * nr

    def q_pair(pr):
        blocks = []
        for gg in range(2):
            g = 2 * pr + gg
            qg = jnp.concatenate([q_ref[:, (g * grp + j) * hd:(g * grp + j + 1) * hd] for j in range(grp)], axis=0)
            zero = jnp.zeros_like(qg)
            blocks.append(jnp.concatenate([qg, zero] if gg == 0 else [zero, qg], axis=1))
        return jnp.concatenate(blocks, axis=0)

    def biased(s, pr, col_bias, row_bias, sign, mask):
        parts = []
        for gg in range(2):
            for j in range(grp):
                h = (2 * pr + gg) * grp + j
                r0 = gg * grows + j * nr
                sj = s[r0:r0 + nr, :] + col_bias[:, h:h + 1] + sign * row_bias[h:h + 1, :]
                parts.append(sj if mask is None else jnp.where(mask, sj, NEG))
        return jnp.concatenate(parts, axis=0)

    @pl.when(c == 0)
    def _():
        m_scr[...] = jnp.full_like(m_scr, NEG)
        l_scr[...] = jnp.zeros_like(l_scr)
        acc_scr[...] = jnp.zeros_like(acc_scr)
        car_scr[...] = jnp.zeros_like(car_scr)
        lf = lfn_ref[...]
        rid = lax.broadcasted_iota(jnp.int32, lf.shape, 0)
        cn = jnp.zeros_like(lf)
        for t in range(nr):
            cn = cn + jnp.where(rid >= t, lf[t:t + 1, :], 0.0)
        cn = cn * LOG2E
        cn_scr[...] = cn
        cnt = _pad_rows(cn, LANES).T
        mask = (lax.broadcasted_iota(jnp.int32, (nr, LANES), 1)
                <= lax.broadcasted_iota(jnp.int32, (nr, LANES), 0))
        for pr in range(nkv // 2):
            kk = _pad_rows(kn_ref[:, 2 * pr * hd:(2 * pr + 2) * hd], LANES)
            vv = _pad_rows(vn_ref[:, 2 * pr * hd:(2 * pr + 2) * hd], LANES)
            s = biased(_dot_nt(q_pair(pr), kk), pr, cn, cnt, -1.0, mask)
            _softmax_step(s, vv, m_scr, l_scr, acc_scr, pr)

    lfs = [lp[p][...] for p in range(npg)]
    pieces = [_split3(x) for x in lfs]
    stacked = jnp.concatenate([pieces[p][k] for k in range(3) for p in range(npg)], axis=0)
    r = lax.broadcasted_iota(jnp.int32, (page, page), 0)
    cc = lax.broadcasted_iota(jnp.int32, (page, page), 1)
    upper = jnp.where(r > cc, 1.0, 0.0).astype(BF16)
    res = _dot(stacked, upper)
    nh = lfs[0].shape[0]
    off = car_scr[...]
    dks = [None] * npg
    for p in range(npg - 1, -1, -1):
        loc = (res[p * nh:(p + 1) * nh] + res[(npg + p) * nh:(npg + p + 1) * nh]
               + res[(2 * npg + p) * nh:(2 * npg + p + 1) * nh])
        dks[p] = (loc + off) * LOG2E
        off = off + (loc[:, 0:1] + lfs[p][:, 0:1])
    car_scr[...] = off
    dk = jnp.concatenate(dks, axis=1)
    cn = cn_scr[...]

    def cached_pair(refs, pr):
        return jnp.concatenate(
            [jnp.concatenate([refs[p][pl.ds(2 * pr + gg, page, stride=nkv), :] for gg in range(2)], axis=1)
             for p in range(npg)], axis=0).astype(BF16)

    for pr in range(nkv // 2):
        s = biased(_dot_nt(q_pair(pr), cached_pair(kp, pr)), pr, cn, dk, 1.0, None)
        _softmax_step(s, cached_pair(vp, pr), m_scr, l_scr, acc_scr, pr)

    @pl.when(c == pl.num_programs(1) - 1)
    def _():
        for pr in range(nkv // 2):
            o = acc_scr[pr] / l_scr[pr]
            for gg in range(2):
                for j in range(grp):
                    h = (2 * pr + gg) * grp + j
                    r0 = gg * grows + j * nr
                    o_ref[:, h * hd:(h + 1) * hd] = o[r0:r0 + nr, gg * hd:(gg + 1) * hd].astype(o_ref.dtype)


def _fox_dec(page_table, q, q_lead, lfn, kn, vn, ck, cv, clf, layer, *, nh, nkv):
    bd, n_pages = page_table.shape
    npg = min(PAGES_PER_STEP, n_pages)
    nc = n_pages // npg
    page = clf.shape[-1]
    hd = LANES
    nr = SUBLANES
    grp = nh // nkv
    assert nkv % 2 == 0
    row = lambda b, c, tab: (b, 0)

    def pmap(p):
        return lambda b, c, tab: (layer, tab[b, (nc - 1 - c) * npg + p], 0, 0)

    in_specs = [pl.BlockSpec((None, nr, nh * hd), lambda b, c, tab: (q_lead, b, 0)),
                pl.BlockSpec((nr, LANES), row), pl.BlockSpec((nr, nkv * hd), row),
                pl.BlockSpec((nr, nkv * hd), row)]
    in_specs += [pl.BlockSpec((None, None, page * nkv, hd), pmap(p)) for p in range(npg)]
    in_specs += [pl.BlockSpec((None, None, page * nkv, hd), pmap(p)) for p in range(npg)]
    in_specs += [pl.BlockSpec((None, None, nh, page), pmap(p)) for p in range(npg)]
    rows = 2 * grp * nr
    return pl.pallas_call(
        functools.partial(_fox_dec_kernel, npg=npg, nkv=nkv, grp=grp, page=page),
        grid_spec=pltpu.PrefetchScalarGridSpec(
            num_scalar_prefetch=1, grid=(bd, nc), in_specs=in_specs,
            out_specs=pl.BlockSpec((nr, nh * hd), row),
            scratch_shapes=[pltpu.VMEM((nkv // 2, rows, 1), F32), pltpu.VMEM((nkv // 2, rows, 1), F32),
                            pltpu.VMEM((nkv // 2, rows, 2 * hd), F32), pltpu.VMEM((nh, 1), F32),
                            pltpu.VMEM((nr, LANES), F32)]),
        out_shape=jax.ShapeDtypeStruct((bd * nr, nh * hd), BF16),
        compiler_params=_cparams("parallel", "arbitrary"),
        name="fox_dec",
    )(page_table, q, lfn, kn, vn, *([ck] * npg), *([cv] * npg), *([clf] * npg))


def _mla_dec_kernel(tab_ref, ql_ref, qa_ref, cn_ref, krn_ref, *rest, npg, nh, ns, rope):
    cp, rp = rest[:npg], rest[npg:2 * npg]
    o_ref = rest[2 * npg]
    m_scr, l_scr, acc_scr = rest[2 * npg + 1:]
    c = pl.program_id(1)
    rows = ns * nh
    ql = ql_ref[...]
    qa = qa_ref[...]

    @pl.when(c == 0)
    def _():
        m_scr[...] = jnp.full_like(m_scr, NEG)
        l_scr[...] = jnp.zeros_like(l_scr)
        acc_scr[...] = jnp.zeros_like(acc_scr)
        cnew = _pad_rows(cn_ref[...], LANES)
        s = _dot_nt(ql, cnew) + _dot_nt(qa, _pad_rows(krn_ref[...], LANES))
        t_id = lax.broadcasted_iota(jnp.int32, (ns, nh, LANES), 0).reshape(rows, LANES)
        s_id = lax.broadcasted_iota(jnp.int32, (rows, LANES), 1)
        _softmax_step(jnp.where(s_id <= t_id, s, NEG), cnew, m_scr, l_scr, acc_scr, 0)

    ckv = jnp.concatenate([cp[p][...] for p in range(npg)], axis=0).astype(BF16)
    krt = jnp.concatenate([rp[p][...] for p in range(npg)], axis=1).astype(BF16)
    s = _dot_nt(ql, ckv) + _dot(qa[:, :rope], krt)
    _softmax_step(s, ckv, m_scr, l_scr, acc_scr, 0)

    @pl.when(c == pl.num_programs(1) - 1)
    def _():
        o_ref[0:rows, :] = (acc_scr[0] / l_scr[0]).astype(o_ref.dtype)
        o_ref[rows:, :] = jnp.zeros((o_ref.shape[0] - rows, o_ref.shape[1]), o_ref.dtype)


def _mla_dec(page_table, qlat, qaux, cnew, krnew, cckv, ckrt, layer, *, nh, ns):
    bd, n_pages = page_table.shape
    npg = min(PAGES_PER_STEP, n_pages)
    nc = n_pages // npg
    page, kvl = cckv.shape[-2], cckv.shape[-1]
    rope = ckrt.shape[-2]
    nr = SUBLANES
    rows = ns * nh
    row = lambda b, c, tab: (b, 0)
    seq = lambda b, c, tab: (b, 0, 0)

    def pmap(p):
        return lambda b, c, tab: (layer, tab[b, c * npg + p], 0, 0)

    in_specs = [pl.BlockSpec((None, rows, kvl), seq), pl.BlockSpec((None, rows, LANES), seq),
                pl.BlockSpec((nr, kvl), row), pl.BlockSpec((nr, LANES), row)]
    in_specs += [pl.BlockSpec((None, None, page, kvl), pmap(p)) for p in range(npg)]
    in_specs += [pl.BlockSpec((None, None, rope, page), pmap(p)) for p in range(npg)]
    return pl.pallas_call(
        functools.partial(_mla_dec_kernel, npg=npg, nh=nh, ns=ns, rope=rope),
        grid_spec=pltpu.PrefetchScalarGridSpec(
            num_scalar_prefetch=1, grid=(bd, nc), in_specs=in_specs,
            out_specs=pl.BlockSpec((None, nr * nh, kvl), seq),
            scratch_shapes=[pltpu.VMEM((1, rows, 1), F32), pltpu.VMEM((1, rows, 1), F32),
                            pltpu.VMEM((1, rows, kvl), F32)]),
        out_shape=jax.ShapeDtypeStruct((bd, nr * nh, kvl), BF16),
        compiler_params=_cparams("parallel", "arbitrary"),
        name="mla_dec",
    )(page_table, qlat, qaux, cnew, krnew, *([cckv] * npg), *([ckrt] * npg))


def _rope_tables(pos, rope):
    half = rope // 2
    inv = 1.0 / (ROPE_THETA ** (jnp.arange(half, dtype=F32) * (2.0 / rope)))
    ang = pos.astype(F32)[:, None] * inv[None, :]
    zeros = jnp.zeros((pos.shape[0], LANES - rope), F32)
    cos = jnp.concatenate([jnp.cos(ang), jnp.cos(ang), zeros], axis=1)
    sin = jnp.concatenate([jnp.sin(ang), jnp.sin(ang), zeros], axis=1)
    return cos, sin


def _rot_cols(w):
    half = w.shape[-1] // 2
    return jnp.concatenate([-w[..., half:], w[..., :half]], axis=-1)


def _pad_last(w, width):
    pad = [(0, 0)] * (w.ndim - 1) + [(0, width - w.shape[-1])]
    return jnp.pad(w, pad)


def kernel(x_prompt, x_sample, cache_fox_k, cache_fox_v, cache_fox_logf, cache_mla_ckv, cache_mla_krope, page_table, norm_fox, w_in_fox, b_f, w_out_fox, norm_mla, w_in_mla, q_norm, w_q_b, kv_norm, w_uk, w_uv, w_out_mla, final_norm):
    bsz, seq, dm = x_prompt.shape
    bd, ns, _ = x_sample.shape
    nf, n_pool, page, nkv, hd = cache_fox_k.shape
    nm = cache_mla_ckv.shape[0]
    nh = b_f.shape[1]
    grp = nh // nkv
    fw, kvw = nh * hd, nkv * hd
    n_pages = page_table.shape[1]
    past = n_pages * page
    _, ql, mh, qk = w_q_b.shape
    kvl, nope = w_uk.shape[1], w_uk.shape[3]
    rope = qk - nope
    vd = w_uv.shape[3]
    assert hd == LANES and nope == LANES and vd == LANES and rope <= LANES and ns <= SUBLANES
    nr = SUBLANES
    mp, ms = bsz * seq, bd * nr

    hp = x_prompt.reshape(mp, dm)
    hs = jnp.pad(x_sample, ((0, 0), (0, nr - ns), (0, 0))).reshape(ms, dm)

    ck = cache_fox_k.reshape(nf, n_pool, page * nkv, hd)
    cv = cache_fox_v.reshape(nf, n_pool, page * nkv, hd)
    clf = jnp.swapaxes(cache_fox_logf, 2, 3)
    ckrt = jnp.swapaxes(cache_mla_krope, 2, 3)

    cos_p, sin_p = _rope_tables(jnp.arange(seq), rope)
    cos_s, sin_s = _rope_tables(past + jnp.arange(nr), rope)
    cos_s, sin_s = jnp.tile(cos_s, (bd, 1)), jnp.tile(sin_s, (bd, 1))
    aux_tables = _fox_aux_tables(nh, nkv)
    fox_scale = hd ** -0.5 * LOG2E
    mla_scale = qk ** -0.5 * LOG2E
    gfin = final_norm.reshape(1, dm)

    outs = {k: [] for k in ("fk_p", "fv_p", "fl_p", "fk_s", "fv_s", "fl_s", "mc_p", "mr_p", "mc_s", "mr_s")}
    depth = nf + nm
    yp = ys = None
    for i in range(depth):
        j = i // 2
        last = i == depth - 1
        if i % 2 == 0:
            w = w_in_fox[j]
            o1, o2, o3, o4 = fw, fw + kvw, fw + 2 * kvw, fw + 2 * kvw + nh
            w_qz = jnp.stack([w[:, :o1], w[:, o4:]]).astype(BF16)
            wk, wv = w[:, o1:o2].astype(BF16), w[:, o2:o3].astype(BF16)
            wf = _pad_last(w[:, o3:o4], LANES).astype(BF16)
            bfp = _pad_last(b_f[j].reshape(1, nh), LANES)
            g = norm_fox[j].reshape(1, dm)
            w_out = w_out_fox[j].astype(BF16)

            qz = _proj_stack(hp, g, w_qz, (fox_scale, 1.0), tm=1024, tn=512, name="fox_qz")
            k, v, lf, kb, vb = _fox_kvf(hp, g, wk, wv, wf, bfp, nh=nh, tm=512)
            qa, ka = _fox_aux(lf.reshape(bsz, seq, LANES), aux_tables, nh=nh, nkv=nkv, tc=512)
            o = _flash(qz.reshape(2, bsz, seq, fw), qa, kb.reshape(bsz, seq, kvw), ka,
                       vb.reshape(bsz, seq, kvw), nkv=nkv, group=grp, tq=256, tk=256,
                       qm_lead=0, km_lead=None, v_lead=None, ka_shared=False, name="fox_flash")
            o = o.reshape(mp, fw)
            outs["fk_p"].append(k.reshape(bsz, seq, nkv, hd))
            outs["fv_p"].append(v.reshape(bsz, seq, nkv, hd))
            outs["fl_p"].append(lf[:, :nh].reshape(bsz, seq, nh))
            if last:
                yp = _out_final(o, qz, 1, w_out, hp, gfin, tm=256)
            else:
                hp = _out_proj(o, qz, 1, w_out, hp, tm=512, tn=512)

            qz = _proj_stack(hs, g, w_qz, (fox_scale, 1.0), tm=1024, tn=512, name="fox_qz_s")
            k, v, lf, kb, vb = _fox_kvf(hs, g, wk, wv, wf, bfp, nh=nh, tm=512)
            o = _fox_dec(page_table, qz, 0, lf, kb, vb, ck, cv, clf, j, nh=nh, nkv=nkv)
            outs["fk_s"].append(k.reshape(bd, nr, nkv, hd)[:, :ns])
            outs["fv_s"].append(v.reshape(bd, nr, nkv, hd)[:, :ns])
            outs["fl_s"].append(lf[:, :nh].reshape(bd, nr, nh)[:, :ns])
            if last:
                ys = _out_final(o, qz, 1, w_out, hs, gfin, tm=256)
            else:
                hs = _out_proj(o, qz, 1, w_out, hs, tm=512, tn=512)
        else:
            w = w_in_mla[j]
            o1, o2, o3 = ql, ql + kvl, ql + kvl + rope
            wqa, wkva = w[:, :o1].astype(BF16), w[:, o1:o2].astype(BF16)
            wkr = w[:, o2:o3]
            wkr2 = jnp.concatenate([_pad_last(wkr, LANES), _pad_last(_rot_cols(wkr), LANES)], axis=1).astype(BF16)
            wz = w[:, o3:].astype(BF16)[None]
            g = norm_mla[j].reshape(1, dm)
            qn, kvn = q_norm[j].reshape(1, ql), kv_norm[j].reshape(1, kvl)
            wqb = w_q_b[j]
            wn = wqb[:, :, :nope].reshape(ql, mh * nope).astype(BF16)
            wr = _pad_last(wqb[:, :, nope:], LANES).reshape(ql, mh * LANES).astype(BF16)
            wrr = _pad_last(_rot_cols(wqb[:, :, nope:]), LANES).reshape(ql, mh * LANES).astype(BF16)
            w_kv = jnp.stack([w_uk[j].reshape(kvl, mh * nope), w_uv[j].reshape(kvl, mh * vd)]).astype(BF16)
            w_ukt = jnp.transpose(w_uk[j], (1, 2, 0)).astype(BF16)
            w_uvh = jnp.transpose(w_uv[j], (1, 0, 2)).astype(BF16)
            w_out = w_out_mla[j].astype(BF16)

            z = _proj_stack(hp, g, wz, (1.0,), tm=1024, tn=512, name="mla_z")
            qan, ckv, ckvb, kr, krp = _mla_a(hp, g, wqa, wkva, wkr2, qn, kvn, cos_p, sin_p, rope=rope, tm=512)
            qm, qx = _mla_q(qan, wn, wr, wrr, cos_p, sin_p, scale=mla_scale, tm=1024, hb=4)
            kvu = _proj_stack(ckvb, None, w_kv, (1.0, 1.0), tm=1024, tn=1024, name="mla_kv")
            o = _flash(qm.reshape(bsz, seq, mh * nope), qx.reshape(bsz, seq, mh * LANES),
                       kvu.reshape(2, bsz, seq, mh * nope), krp.reshape(bsz, seq, LANES),
                       kvu.reshape(2, bsz, seq, mh * vd), nkv=mh, group=1, tq=512, tk=256,
                       qm_lead=None, km_lead=0, v_lead=1, ka_shared=True, name="mla_flash")
            o = o.reshape(mp, mh * vd)
            outs["mc_p"].append(ckv.reshape(bsz, seq, kvl))
            outs["mr_p"].append(kr.reshape(bsz, seq, rope))
            if last:
                yp = _out_final(o, z, 0, w_out, hp, gfin, tm=256)
            else:
                hp = _out_proj(o, z, 0, w_out, hp, tm=512, tn=512)

            z = _proj_stack(hs, g, wz, (1.0,), tm=1024, tn=512, name="mla_z_s")
            qan, ckv, ckvb, kr, krp = _mla_a(hs, g, wqa, wkva, wkr2, qn, kvn, cos_s, sin_s, rope=rope, tm=512)
            qm, qx = _mla_q(qan, wn, wr, wrr, cos_s, sin_s, scale=mla_scale, tm=1024, hb=4)
            qlat = _head_mm(qm, w_ukt, "mla_absorb")
            olat = _mla_dec(page_table, qlat.reshape(bd, nr * mh, kvl), qx.reshape(bd, nr * mh, LANES),
                            ckvb, krp, cache_mla_ckv, ckrt, j, nh=mh, ns=ns)
            o = _head_mm(olat.reshape(ms, mh * kvl), w_uvh, "mla_uv")
            outs["mc_s"].append(ckv.reshape(bd, nr, kvl)[:, :ns])
            outs["mr_s"].append(kr.reshape(bd, nr, rope)[:, :ns])
            if last:
                ys = _out_final(o, z, 0, w_out, hs, gfin, tm=256)
            else:
                hs = _out_proj(o, z, 0, w_out, hs, tm=512, tn=512)

    y_prompt = yp.reshape(bsz, seq, dm)
    y_sample = ys.reshape(bd, nr, dm)[:, :ns]
    st = lambda key: jnp.stack(outs[key])
    return (y_prompt, y_sample, st("fk_p"), st("fv_p"), st("fl_p"), st("fk_s"), st("fv_s"), st("fl_s"),
            st("mc_p"), st("mr_p"), st("mc_s"), st("mr_s"))
```

```python
import functools

import numpy as np
import jax
import jax.numpy as jnp
from jax import lax
from jax.experimental import pallas as pl
from jax.experimental.pallas import tpu as pltpu

F32 = jnp.float32
BF16 = jnp.bfloat16
EPS = 1e-6
ROPE_THETA = 10000.0
LOG2E = 1.4426950408889634
NEG = -1e30
LANES = 128
SUBLANES = 8
VMEM_LIMIT = 52 * 1024 * 1024
PAGES_PER_STEP = 8


def _cparams(*sem):
    return pltpu.CompilerParams(dimension_semantics=sem, vmem_limit_bytes=VMEM_LIMIT)


def _rms(x, g):
    return x * lax.rsqrt(jnp.mean(x * x, axis=-1, keepdims=True) + EPS) * g


def _dot(a, b):
    return jnp.dot(a, b, preferred_element_type=F32)


def _dot_nt(a, b):
    return lax.dot_general(a, b, (((1,), (1,)), ((), ())), preferred_element_type=F32)


def _split3(x):
    hi = x.astype(BF16)
    r1 = x - hi.astype(F32)
    mid = r1.astype(BF16)
    lo = (r1 - mid.astype(F32)).astype(BF16)
    return hi, mid, lo


def _log_sigmoid(x):
    return jnp.minimum(x, 0.0) - jnp.log1p(jnp.exp(-jnp.abs(x)))


def _proj_kernel(*refs, norm, scales):
    if norm:
        x_ref, g_ref, w_ref, o_ref, h_ref = refs
    else:
        x_ref, w_ref, o_ref = refs
    gi = pl.program_id(1)
    if norm:
        @pl.when((gi == 0) & (pl.program_id(2) == 0))
        def _():
            h_ref[...] = _rms(x_ref[...], g_ref[...]).astype(BF16)
        h = h_ref[...]
    else:
        h = x_ref[...]
    acc = _dot(h, w_ref[...])
    if any(s != 1.0 for s in scales):
        sc = jnp.float32(scales[-1])
        for idx in range(len(scales) - 2, -1, -1):
            sc = jnp.where(gi == idx, jnp.float32(scales[idx]), sc)
        acc = acc * sc
    o_ref[...] = acc.astype(o_ref.dtype)


def _proj_stack(x, g, w, scales, *, tm, tn, name):
    m, kd = x.shape
    ng, _, n = w.shape
    norm = g is not None
    tm, tn = min(tm, m), min(tn, n)
    in_specs = [pl.BlockSpec((tm, kd), lambda i, k, j: (i, 0))]
    args = [x]
    scratch = []
    if norm:
        in_specs.append(pl.BlockSpec((1, kd), lambda i, k, j: (0, 0)))
        args.append(g)
        scratch.append(pltpu.VMEM((tm, kd), BF16))
    in_specs.append(pl.BlockSpec((None, kd, tn), lambda i, k, j: (k, 0, j)))
    args.append(w)
    return pl.pallas_call(
        functools.partial(_proj_kernel, norm=norm, scales=tuple(scales)),
        grid=(m // tm, ng, n // tn),
        in_specs=in_specs,
        out_specs=pl.BlockSpec((None, tm, tn), lambda i, k, j: (k, i, j)),
        out_shape=jax.ShapeDtypeStruct((ng, m, n), BF16),
        scratch_shapes=scratch,
        compiler_params=_cparams("parallel", "arbitrary", "arbitrary"),
        name=name,
    )(*args)


def _fox_kvf_kernel(x_ref, g_ref, wk_ref, wv_ref, wf_ref, bf_ref,
                    k_ref, v_ref, lf_ref, kb_ref, vb_ref, *, nh):
    h = _rms(x_ref[...], g_ref[...]).astype(BF16)
    k = _dot(h, wk_ref[...])
    v = _dot(h, wv_ref[...])
    tm = k.shape[0]
    nkv = k.shape[1] // LANES
    for g in range(nkv):
        k_ref[pl.ds(g, tm, stride=nkv), :] = k[:, g * LANES:(g + 1) * LANES]
        v_ref[pl.ds(g, tm, stride=nkv), :] = v[:, g * LANES:(g + 1) * LANES]
    kb_ref[...] = k.astype(BF16)
    vb_ref[...] = v.astype(BF16)
    lf = _log_sigmoid(_dot(h, wf_ref[...]) + bf_ref[...])
    lane = lax.broadcasted_iota(jnp.int32, lf.shape, 1)
    lf_ref[...] = jnp.where(lane < nh, lf, 0.0)


def _fox_kvf(x, g, wk, wv, wf, bf, *, nh, tm):
    m, kd = x.shape
    kvw = wk.shape[1]
    nkv = kvw // LANES
    tm = min(tm, m)
    row = lambda i: (i, 0)
    fix = lambda i: (0, 0)
    return pl.pallas_call(
        functools.partial(_fox_kvf_kernel, nh=nh),
        grid=(m // tm,),
        in_specs=[pl.BlockSpec((tm, kd), row), pl.BlockSpec((1, kd), fix),
                  pl.BlockSpec((kd, kvw), fix), pl.BlockSpec((kd, kvw), fix),
                  pl.BlockSpec((kd, LANES), fix), pl.BlockSpec((1, LANES), fix)],
        out_specs=[pl.BlockSpec((tm * nkv, LANES), row), pl.BlockSpec((tm * nkv, LANES), row),
                   pl.BlockSpec((tm, LANES), row), pl.BlockSpec((tm, kvw), row),
                   pl.BlockSpec((tm, kvw), row)],
        out_shape=[jax.ShapeDtypeStruct((m * nkv, LANES), F32), jax.ShapeDtypeStruct((m * nkv, LANES), F32),
                   jax.ShapeDtypeStruct((m, LANES), F32), jax.ShapeDtypeStruct((m, kvw), BF16),
                   jax.ShapeDtypeStruct((m, kvw), BF16)],
        compiler_params=_cparams("parallel"),
        name="fox_kvf",
    )(x, g, wk, wv, wf, bf)


def _fox_aux_kernel(lf_ref, sq_ref, qc_ref, sk_ref, kc_ref, qa_ref, ka_ref, carry_ref):
    @pl.when(pl.program_id(1) == 0)
    def _():
        carry_ref[...] = jnp.zeros_like(carry_ref)
    lf = lf_ref[...]
    tc = lf.shape[0]
    r = lax.broadcasted_iota(jnp.int32, (tc, tc), 0)
    c = lax.broadcasted_iota(jnp.int32, (tc, tc), 1)
    tri = jnp.where(r >= c, 1.0, 0.0).astype(BF16)
    p = _split3(lf)
    cc = carry_ref[...] + (_dot(tri, p[0]) + _dot(tri, p[1]) + _dot(tri, p[2]))
    carry_ref[...] = cc[tc - 1:tc, :]
    p2 = _split3(cc * LOG2E)
    qa = qc_ref[...] + (_dot(p2[0], sq_ref[0]) + _dot(p2[1], sq_ref[1]) + _dot(p2[2], sq_ref[2]))
    ka = kc_ref[...] + (_dot(p2[0], sk_ref[0]) + _dot(p2[1], sk_ref[1]) + _dot(p2[2], sk_ref[2]))
    qa_ref[...] = qa.astype(BF16)
    ka_ref[...] = ka.astype(BF16)


def _fox_aux_tables(nh, nkv):
    grp = nh // nkv
    sq = np.zeros((3, LANES, nh * LANES), np.float32)
    qc = np.zeros((1, nh * LANES), np.float32)
    sk = np.zeros((3, LANES, nkv * LANES), np.float32)
    kc = np.zeros((1, nkv * LANES), np.float32)
    for h in range(nh):
        g, j = divmod(h, grp)
        for piece in range(3):
            sq[piece, h, h * LANES + piece] = 1.0
            qc[0, h * LANES + 3 + 3 * j + piece] = 1.0
            sk[piece, h, g * LANES + 3 + 3 * j + piece] = -1.0
    for g in range(nkv):
        kc[0, g * LANES:g * LANES + 3] = 1.0
    return (jnp.asarray(sq, BF16), jnp.asarray(qc, F32), jnp.asarray(sk, BF16), jnp.asarray(kc, F32))


def _fox_aux(lf, tables, *, nh, nkv, tc):
    b, s, _ = lf.shape
    sq, qc, sk, kc = tables
    tc = min(tc, s)
    return pl.pallas_call(
        _fox_aux_kernel,
        grid=(b, s // tc),
        in_specs=[pl.BlockSpec((None, tc, LANES), lambda bi, i: (bi, i, 0)),
                  pl.BlockSpec(sq.shape, lambda bi, i: (0, 0, 0)),
                  pl.BlockSpec(qc.shape, lambda bi, i: (0, 0)),
                  pl.BlockSpec(sk.shape, lambda bi, i: (0, 0, 0)),
                  pl.BlockSpec(kc.shape, lambda bi, i: (0, 0))],
        out_specs=[pl.BlockSpec((None, tc, nh * LANES), lambda bi, i: (bi, i, 0)),
                   pl.BlockSpec((None, tc, nkv * LANES), lambda bi, i: (bi, i, 0))],
        out_shape=[jax.ShapeDtypeStruct((b, s, nh * LANES), BF16),
                   jax.ShapeDtypeStruct((b, s, nkv * LANES), BF16)],
        scratch_shapes=[pltpu.VMEM((1, LANES), F32)],
        compiler_params=_cparams("parallel", "arbitrary"),
        name="fox_aux",
    )(lf, sq, qc, sk, kc)


def _silu(z):
    return z * (1.0 / (1.0 + jnp.exp(-z)))


def _flash_kernel(qm_ref, qa_ref, km_ref, ka_ref, v_ref, z_ref, o_ref,
                  q_scr, s0_scr, s1_scr, m_scr, acc_scr, *, group, tq, tk, hd):
    i = pl.program_id(2)
    rows = group * tq
    per = tq // tk
    s_bufs = (s0_scr, s1_scr)
    for j in range(group):
        q_scr[j * tq:(j + 1) * tq, 0:hd] = qm_ref[:, j * hd:(j + 1) * hd]
        q_scr[j * tq:(j + 1) * tq, hd:2 * hd] = qa_ref[:, j * hd:(j + 1) * hd]
    m_scr[...] = jnp.full_like(m_scr, NEG)
    acc_scr[...] = jnp.zeros_like(acc_scr)
    ones = jnp.ones((tk, hd), BF16)

    def scores(step, slot, mask_off):
        ks = pl.multiple_of(step * tk, tk)
        k = jnp.concatenate([km_ref[pl.ds(ks, tk), :], ka_ref[pl.ds(ks, tk), :]], axis=1)
        s = _dot_nt(q_scr[...], k)
        if mask_off is not None:
            t = lax.broadcasted_iota(jnp.int32, (rows, tk), 0) & (tq - 1)
            col = lax.broadcasted_iota(jnp.int32, (rows, tk), 1) + mask_off
            s = jnp.where(col <= t, s, NEG)
        s_bufs[slot][...] = s

    def consume(step, slot):
        ks = pl.multiple_of(step * tk, tk)
        s_ref = s_bufs[slot]
        m_prev = m_scr[...]
        m_new = jnp.maximum(m_prev, jnp.max(s_ref[...], axis=1, keepdims=True))
        alpha = jnp.exp2(m_prev - m_new)
        p = jnp.exp2(s_ref[...] - jnp.tile(m_new, (1, tk // hd)))
        v1 = jnp.concatenate([v_ref[pl.ds(ks, tk), :], ones], axis=1)
        acc_scr[...] = jnp.tile(alpha, (1, 2)) * acc_scr[...] + _dot(p.astype(BF16), v1)
        m_scr[...] = m_new

    def tail(first, n_plain):
        seq = [(first + n, None) for n in range(n_plain)]
        seq += [(first + n_plain + d, d * tk) for d in range(per)]
        for n, (step, _) in enumerate(seq):
            if n + 1 < len(seq):
                scores(seq[n + 1][0], (n + 1) % 2, seq[n + 1][1])
            consume(step, n % 2)

    n_plain = i * per

    @pl.when(i == 0)
    def _():
        scores(0, 0, 0)
        tail(0, 0)

    @pl.when(i > 0)
    def _():
        scores(0, 0, None)
        n_pairs = (n_plain - 1) // 2

        def body(jj, carry):
            scores(2 * jj + 1, 1, None)
            consume(2 * jj, 0)
            scores(2 * jj + 2, 0, None)
            consume(2 * jj + 1, 1)
            return carry

        lax.fori_loop(0, n_pairs, body, 0)
        if per % 2 == 0:
            tail(2 * n_pairs, 2)
        else:
            @pl.when(n_plain % 2 == 1)
            def _():
                tail(2 * n_pairs, 1)

            @pl.when(n_plain % 2 == 0)
            def _():
                tail(2 * n_pairs, 2)

    acc = acc_scr[...]
    o = acc[:, :hd] / acc[:, hd:]
    for j in range(group):
        gate = _silu(z_ref[:, j * hd:(j + 1) * hd].astype(F32))
        o_ref[:, j * hd:(j + 1) * hd] = (o[j * tq:(j + 1) * tq, :] * gate).astype(o_ref.dtype)


def _flash(qm, qa, km, ka, v, z, *, nkv, group, tq, tk, qm_lead, km_lead, v_lead, z_lead, ka_shared, name):
    hd = LANES
    b, s = qa.shape[0], qa.shape[1]
    tq, tk = min(tq, s), min(tk, s)

    def spec(lead, blk_rows, width, imap):
        if lead is None:
            return pl.BlockSpec((None, blk_rows, width), imap)
        return pl.BlockSpec((None, None, blk_rows, width), lambda bi, g, i: (lead,) + imap(bi, g, i))

    q_map = lambda bi, g, i: (bi, i, g)
    k_map = lambda bi, g, i: (bi, 0, g)
    ka_map = (lambda bi, g, i: (bi, 0, 0)) if ka_shared else k_map
    return pl.pallas_call(
        functools.partial(_flash_kernel, group=group, tq=tq, tk=tk, hd=hd),
        grid=(b, nkv, s // tq),
        in_specs=[spec(qm_lead, tq, group * hd, q_map), spec(None, tq, group * hd, q_map),
                  spec(km_lead, s, hd, k_map), spec(None, s, hd, ka_map), spec(v_lead, s, hd, k_map),
                  spec(z_lead, tq, group * hd, q_map)],
        out_specs=pl.BlockSpec((None, tq, group * hd), q_map),
        out_shape=jax.ShapeDtypeStruct((b, s, nkv * group * hd), BF16),
        scratch_shapes=[pltpu.VMEM((group * tq, 2 * hd), BF16), pltpu.VMEM((group * tq, tk), F32),
                        pltpu.VMEM((group * tq, tk), F32), pltpu.VMEM((group * tq, hd), F32),
                        pltpu.VMEM((group * tq, 2 * hd), F32)],
        compiler_params=_cparams("parallel", "parallel", "arbitrary"),
        name=name,
    )(qm, qa, km, ka, v, z)


def _gated(o_ref, z_ref):
    if z_ref is None:
        return o_ref[...]
    return (o_ref[...].astype(F32) * _silu(z_ref[...].astype(F32))).astype(BF16)


def _out_proj_kernel(*refs, gated):
    if gated:
        o_ref, z_ref, w_ref, r_ref, y_ref, u_scr = refs

        @pl.when(pl.program_id(1) == 0)
        def _():
            u_scr[...] = _gated(o_ref, z_ref)
        u = u_scr[...]
    else:
        o_ref, w_ref, r_ref, y_ref = refs
        u = o_ref[...]
    y_ref[...] = r_ref[...] + _dot(u, w_ref[...])


def _out_proj(o, z, z_lead, w, resid, *, tm, tn):
    m, kd = o.shape
    n = w.shape[1]
    tm, tn = min(tm, m), min(tn, n)
    gated = z is not None
    in_specs = [pl.BlockSpec((tm, kd), lambda i, j: (i, 0))]
    args = [o]
    if gated:
        in_specs.append(pl.BlockSpec((None, tm, kd), lambda i, j: (z_lead, i, 0)))
        args.append(z)
    in_specs += [pl.BlockSpec((kd, tn), lambda i, j: (0, j)), pl.BlockSpec((tm, tn), lambda i, j: (i, j))]
    return pl.pallas_call(
        functools.partial(_out_proj_kernel, gated=gated),
        grid=(m // tm, n // tn),
        in_specs=in_specs,
        out_specs=pl.BlockSpec((tm, tn), lambda i, j: (i, j)),
        out_shape=jax.ShapeDtypeStruct((m, n), F32),
        scratch_shapes=[pltpu.VMEM((tm, kd), BF16)] if gated else [],
        compiler_params=_cparams("parallel", "arbitrary"),
        name="out_proj",
    )(*args, w, resid)


def _out_final_kernel(*refs, gated):
    if gated:
        o_ref, z_ref, w_ref, r_ref, g_ref, y_ref = refs
    else:
        o_ref, w_ref, r_ref, g_ref, y_ref = refs
        z_ref = None
    hsum = r_ref[...] + _dot(_gated(o_ref, z_ref), w_ref[...])
    y_ref[...] = _rms(hsum, g_ref[...])


def _out_final(o, z, z_lead, w, resid, gfin, *, tm):
    m, kd = o.shape
    n = w.shape[1]
    tm = min(tm, m)
    gated = z is not None
    in_specs = [pl.BlockSpec((tm, kd), lambda i: (i, 0))]
    args = [o]
    if gated:
        in_specs.append(pl.BlockSpec((None, tm, kd), lambda i: (z_lead, i, 0)))
        args.append(z)
    in_specs += [pl.BlockSpec((kd, n), lambda i: (0, 0)), pl.BlockSpec((tm, n), lambda i: (i, 0)),
                 pl.BlockSpec((1, n), lambda i: (0, 0))]
    return pl.pallas_call(
        functools.partial(_out_final_kernel, gated=gated),
        grid=(m // tm,),
        in_specs=in_specs,
        out_specs=pl.BlockSpec((tm, n), lambda i: (i, 0)),
        out_shape=jax.ShapeDtypeStruct((m, n), F32),
        compiler_params=_cparams("parallel"),
        name="out_final",
    )(*args, w, resid, gfin)


def _mla_a_kernel(x_ref, g_ref, wqa_ref, wkva_ref, wkr_ref, qn_ref, kvn_ref, cos_ref, sin_ref,
                  qa_ref, ckv_ref, ckvb_ref, kr_ref, krp_ref, *, rope):
    h = _rms(x_ref[...], g_ref[...]).astype(BF16)
    qa_ref[...] = _rms(_dot(h, wqa_ref[...]), qn_ref[...]).astype(BF16)
    ckv = _rms(_dot(h, wkva_ref[...]), kvn_ref[...])
    ckv_ref[...] = ckv
    ckvb_ref[...] = ckv.astype(BF16)
    kr2 = _dot(h, wkr_ref[...])
    krp = kr2[:, :LANES] * cos_ref[...] + kr2[:, LANES:] * sin_ref[...]
    kr_ref[...] = krp[:, :rope]
    krp_ref[...] = krp.astype(BF16)


def _mla_a(x, g, wqa, wkva, wkr2, qn, kvn, cos, sin, *, rope, tm):
    m, kd = x.shape
    ql, kvl = wqa.shape[1], wkva.shape[1]
    tm = min(tm, m)
    nt = cos.shape[0] // tm
    row = lambda i: (i, 0)
    fix = lambda i: (0, 0)
    tab = lambda i: (i % nt, 0)
    return pl.pallas_call(
        functools.partial(_mla_a_kernel, rope=rope),
        grid=(m // tm,),
        in_specs=[pl.BlockSpec((tm, kd), row), pl.BlockSpec((1, kd), fix),
                  pl.BlockSpec((kd, ql), fix), pl.BlockSpec((kd, kvl), fix),
                  pl.BlockSpec((kd, 2 * LANES), fix), pl.BlockSpec((1, ql), fix),
                  pl.BlockSpec((1, kvl), fix), pl.BlockSpec((tm, LANES), tab),
                  pl.BlockSpec((tm, LANES), tab)],
        out_specs=[pl.BlockSpec((tm, ql), row), pl.BlockSpec((tm, kvl), row),
                   pl.BlockSpec((tm, kvl), row), pl.BlockSpec((tm, rope), row),
                   pl.BlockSpec((tm, LANES), row)],
        out_shape=[jax.ShapeDtypeStruct((m, ql), BF16), jax.ShapeDtypeStruct((m, kvl), F32),
                   jax.ShapeDtypeStruct((m, kvl), BF16), jax.ShapeDtypeStruct((m, rope), F32),
                   jax.ShapeDtypeStruct((m, LANES), BF16)],
        compiler_params=_cparams("parallel"),
        name="mla_a",
    )(x, g, wqa, wkva, wkr2, qn, kvn, cos, sin)


def _mla_q_kernel(x_ref, wn_ref, wr_ref, wrr_ref, cos_ref, sin_ref, qm_ref, qa_ref, *, hb, scale):
    x = x_ref[...]
    qm_ref[...] = (_dot(x, wn_ref[...]) * scale).astype(BF16)
    qr = _dot(x, wr_ref[...])
    qrr = _dot(x, wrr_ref[...])
    cos = cos_ref[...]
    sin = sin_ref[...]
    for jj in range(hb):
        sl = slice(jj * LANES, (jj + 1) * LANES)
        qa_ref[:, sl] = ((qr[:, sl] * cos + qrr[:, sl] * sin) * scale).astype(BF16)


def _mla_q(x, wn, wr, wrr, cos, sin, *, scale, tm, hb):
    m, kd = x.shape
    n = wn.shape[1]
    tm = min(tm, m)
    tn = hb * LANES
    nt = cos.shape[0] // tm
    wmap = lambda i, j: (0, j)
    omap = lambda i, j: (i, j)
    tab = lambda i, j: (i % nt, 0)
    return pl.pallas_call(
        functools.partial(_mla_q_kernel, hb=hb, scale=scale),
        grid=(m // tm, n // tn),
        in_specs=[pl.BlockSpec((tm, kd), lambda i, j: (i, 0)),
                  pl.BlockSpec((kd, tn), wmap), pl.BlockSpec((kd, tn), wmap), pl.BlockSpec((kd, tn), wmap),
                  pl.BlockSpec((tm, LANES), tab), pl.BlockSpec((tm, LANES), tab)],
        out_specs=[pl.BlockSpec((tm, tn), omap), pl.BlockSpec((tm, tn), omap)],
        out_shape=[jax.ShapeDtypeStruct((m, n), BF16), jax.ShapeDtypeStruct((m, n), BF16)],
        compiler_params=_cparams("parallel", "arbitrary"),
        name="mla_q",
    )(x, wn, wr, wrr, cos, sin)


def _absorb_kernel(x_ref, w_ref, o_ref, *, nh):
    tm = x_ref.shape[0]
    kd = x_ref.shape[1] // nh
    for h in range(nh):
        y = _dot(x_ref[:, h * kd:(h + 1) * kd], w_ref[h])
        for c in range(o_ref.shape[0]):
            o_ref[c, pl.ds(h, tm, stride=nh), :] = y[:, c * LANES:(c + 1) * LANES]


def _absorb(x, w, *, tm):
    m = x.shape[0]
    nh, kd, n = w.shape
    tm = min(tm, m)
    return pl.pallas_call(
        functools.partial(_absorb_kernel, nh=nh),
        grid=(m // tm,),
        in_specs=[pl.BlockSpec((tm, nh * kd), lambda i: (i, 0)), pl.BlockSpec((nh, kd, n), lambda i: (0, 0, 0))],
        out_specs=pl.BlockSpec((n // LANES, tm * nh, LANES), lambda i: (0, i, 0)),
        out_shape=jax.ShapeDtypeStruct((n // LANES, m * nh, LANES), F32),
        compiler_params=_cparams("parallel"),
        name="mla_absorb",
    )(x, w)


def _unabsorb_kernel(x_ref, w_ref, o_ref, *, nh):
    tm = o_ref.shape[0]
    n = o_ref.shape[1] // nh
    for h in range(nh):
        xh = jnp.concatenate([x_ref[c, pl.ds(h, tm, stride=nh), :] for c in range(x_ref.shape[0])], axis=1)
        o_ref[:, h * n:(h + 1) * n] = _dot(xh.astype(BF16), w_ref[h]).astype(o_ref.dtype)


def _unabsorb(x, w, *, tm):
    nh, kd, n = w.shape
    m = x.shape[1] // nh
    tm = min(tm, m)
    return pl.pallas_call(
        functools.partial(_unabsorb_kernel, nh=nh),
        grid=(m // tm,),
        in_specs=[pl.BlockSpec((kd // LANES, tm * nh, LANES), lambda i: (0, i, 0)),
                  pl.BlockSpec((nh, kd, n), lambda i: (0, 0, 0))],
        out_specs=pl.BlockSpec((tm, nh * n), lambda i: (i, 0)),
        out_shape=jax.ShapeDtypeStruct((m, nh * n), BF16),
        compiler_params=_cparams("parallel"),
        name="mla_uv",
    )(x, w)


def _softmax_step(s, v, m_scr, l_scr, acc_scr, idx):
    m_prev = m_scr[idx]
    m_new = jnp.maximum(m_prev, jnp.max(s, axis=1, keepdims=True))
    alpha = jnp.exp2(m_prev - m_new)
    p = jnp.exp2(s - m_new)
    l_scr[idx] = alpha * l_scr[idx] + jnp.sum(p, axis=1, keepdims=True)
    acc_scr[idx] = alpha * acc_scr[idx] + _dot(p.astype(BF16), v)
    m_scr[idx] = m_new


def _pad_rows(x, rows):
    return jnp.concatenate([x, jnp.zeros((rows - x.shape[0], x.shape[1]), x.dtype)], axis=0)


def _page_copies(tab_ref, seq, first_page, layer, srcs, bufs, sems, slot, npg):
    copies = []
    for p in range(npg):
        pg = tab_ref[seq, first_page + p]
        for a, (src, buf) in enumerate(zip(srcs, bufs)):
            copies.append(pltpu.make_async_copy(src.at[layer, pg], buf.at[slot, p], sems.at[a, slot]))
    return copies


def _paged_pipeline(tab_ref, srcs, bufs, sems, *, layer, npg, nc, reverse, on_chunk):
    b, c2 = pl.program_id(0), pl.program_id(1)
    total = pl.num_programs(0) * nc
    n0 = (b * (nc // 2) + c2) * 2

    def copies(n, slot):
        seq, cc = n // nc, n % nc
        first = ((nc - 1 - cc) if reverse else cc) * npg
        return _page_copies(tab_ref, seq, first, layer, srcs, bufs, sems, slot, npg)

    @pl.when(n0 == 0)
    def _():
        for slot in range(2):
            for cp in copies(slot, slot):
                cp.start()

    for half in range(2):
        n = n0 + half
        for cp in copies(n, half):
            cp.wait()
        on_chunk(half)

        @pl.when(n + 2 < total)
        def _():
            for cp in copies(n + 2, half):
                cp.start()


def _fox_dec_kernel(tab_ref, q_ref, lfn_ref, kn_ref, vn_ref, ck_ref, cv_ref, clf_ref, o_ref,
                    kbuf, vbuf, lbuf, sems, m_scr, l_scr, acc_scr, car_scr, cn_scr,
                    *, layer, npg, nc, nkv, grp, page):
    c2 = pl.program_id(1)
    hd = LANES
    nr = SUBLANES
    grows = grp * nr

    def q_pair(pr):
        blocks = []
        for gg in range(2):
            g = 2 * pr + gg
            qg = jnp.concatenate([q_ref[:, (g * grp + j) * hd:(g * grp + j + 1) * hd] for j in range(grp)], axis=0)
            zero = jnp.zeros_like(qg)
            blocks.append(jnp.concatenate([qg, zero] if gg == 0 else [zero, qg], axis=1))
        return jnp.concatenate(blocks, axis=0)

    def biased(s, pr, col_bias, row_bias, sign, mask):
        parts = []
        for gg in range(2):
            for j in range(grp):
                h = (2 * pr + gg) * grp + j
                r0 = gg * grows + j * nr
                sj = s[r0:r0 + nr, :] + col_bias[:, h:h + 1] + sign * row_bias[h:h + 1, :]
                parts.append(sj if mask is None else jnp.where(mask, sj, NEG))
        return jnp.concatenate(parts, axis=0)

    @pl.when(c2 == 0)
    def _():
        m_scr[...] = jnp.full_like(m_scr, NEG)
        l_scr[...] = jnp.zeros_like(l_scr)
        acc_scr[...] = jnp.zeros_like(acc_scr)
        car_scr[...] = jnp.zeros_like(car_scr)
        lf = lfn_ref[...]
        rid = lax.broadcasted_iota(jnp.int32, lf.shape, 0)
        cn = jnp.zeros_like(lf)
        for t in range(nr):
            cn = cn + jnp.where(rid >= t, lf[t:t + 1, :], 0.0)
        cn = cn * LOG2E
        cn_scr[...] = cn
        cnt = _pad_rows(cn, LANES).T
        mask = (lax.broadcasted_iota(jnp.int32, (nr, LANES), 1)
                <= lax.broadcasted_iota(jnp.int32, (nr, LANES), 0))
        for pr in range(nkv // 2):
            kk = _pad_rows(kn_ref[:, 2 * pr * hd:(2 * pr + 2) * hd], LANES)
            vv = _pad_rows(vn_ref[:, 2 * pr * hd:(2 * pr + 2) * hd], LANES)
            s = biased(_dot_nt(q_pair(pr), kk), pr, cn, cnt, -1.0, mask)
            _softmax_step(s, vv, m_scr, l_scr, acc_scr, pr)

    def on_chunk(slot):
        lfs = [lbuf[slot, p] for p in range(npg)]
        pieces = [_split3(x) for x in lfs]
        stacked = jnp.concatenate([pieces[p][k] for k in range(3) for p in range(npg)], axis=0)
        r = lax.broadcasted_iota(jnp.int32, (page, page), 0)
        cc = lax.broadcasted_iota(jnp.int32, (page, page), 1)
        upper = jnp.where(r > cc, 1.0, 0.0).astype(BF16)
        res = _dot(stacked, upper)
        nh = lfs[0].shape[0]
        off = car_scr[...]
        dks = [None] * npg
        for p in range(npg - 1, -1, -1):
            loc = (res[p * nh:(p + 1) * nh] + res[(npg + p) * nh:(npg + p + 1) * nh]
                   + res[(2 * npg + p) * nh:(2 * npg + p + 1) * nh])
            dks[p] = (loc + off) * LOG2E
            off = off + (loc[:, 0:1] + lfs[p][:, 0:1])
        car_scr[...] = off
        dk = jnp.concatenate(dks, axis=1)
        cn = cn_scr[...]

        def cached_pair(buf, pr):
            return jnp.concatenate(
                [jnp.concatenate([buf[slot, p, pl.ds(2 * pr + gg, page, stride=nkv), :] for gg in range(2)], axis=1)
                 for p in range(npg)], axis=0).astype(BF16)

        for pr in range(nkv // 2):
            s = biased(_dot_nt(q_pair(pr), cached_pair(kbuf, pr)), pr, cn, dk, 1.0, None)
            _softmax_step(s, cached_pair(vbuf, pr), m_scr, l_scr, acc_scr, pr)

    _paged_pipeline(tab_ref, (ck_ref, cv_ref, clf_ref), (kbuf, vbuf, lbuf), sems,
                    layer=layer, npg=npg, nc=nc, reverse=True, on_chunk=on_chunk)

    @pl.when(c2 == pl.num_programs(1) - 1)
    def _():
        for pr in range(nkv // 2):
            o = acc_scr[pr] / l_scr[pr]
            for gg in range(2):
                for j in range(grp):
                    h = (2 * pr + gg) * grp + j
                    r0 = gg * grows + j * nr
                    o_ref[:, h * hd:(h + 1) * hd] = o[r0:r0 + nr, gg * hd:(gg + 1) * hd].astype(o_ref.dtype)


def _fox_dec(page_table, q, q_lead, lfn, kn, vn, ck, cv, clf, layer, *, nh, nkv):
    bd, n_pages = page_table.shape
    npg = min(PAGES_PER_STEP, n_pages)
    nc = n_pages // npg
    assert nc % 2 == 0 and nkv % 2 == 0
    page = clf.shape[-1]
    hd = LANES
    nr = SUBLANES
    grp = nh // nkv
    row = lambda b, c, tab: (b, 0)
    hbm = pl.BlockSpec(memory_space=pl.ANY)
    in_specs = [pl.BlockSpec((None, nr, nh * hd), lambda b, c, tab: (q_lead, b, 0)),
                pl.BlockSpec((nr, LANES), row), pl.BlockSpec((nr, nkv * hd), row),
                pl.BlockSpec((nr, nkv * hd), row), hbm, hbm, hbm]
    rows = 2 * grp * nr
    return pl.pallas_call(
        functools.partial(_fox_dec_kernel, layer=layer, npg=npg, nc=nc, nkv=nkv, grp=grp, page=page),
        grid_spec=pltpu.PrefetchScalarGridSpec(
            num_scalar_prefetch=1, grid=(bd, nc // 2), in_specs=in_specs,
            out_specs=pl.BlockSpec((nr, nh * hd), row),
            scratch_shapes=[pltpu.VMEM((2, npg, page * nkv, hd), F32), pltpu.VMEM((2, npg, page * nkv, hd), F32),
                            pltpu.VMEM((2, npg, nh, page), F32), pltpu.SemaphoreType.DMA((3, 2)),
                            pltpu.VMEM((nkv // 2, rows, 1), F32), pltpu.VMEM((nkv // 2, rows, 1), F32),
                            pltpu.VMEM((nkv // 2, rows, 2 * hd), F32), pltpu.VMEM((nh, 1), F32),
                            pltpu.VMEM((nr, LANES), F32)]),
        out_shape=jax.ShapeDtypeStruct((bd * nr, nh * hd), BF16),
        compiler_params=_cparams("arbitrary", "arbitrary"),
        name="fox_dec",
    )(page_table, q, lfn, kn, vn, ck, cv, clf)


def _mla_dec_kernel(tab_ref, ql_ref, qa_ref, cn_ref, krn_ref, cc_ref, cr_ref, o_ref,
                    cbuf, rbuf, sems, m_scr, l_scr, acc_scr, *, layer, npg, nc, nh, ns, rope):
    c2 = pl.program_id(1)
    rows = ns * nh
    nslab = ql_ref.shape[0]
    ql = jnp.concatenate([ql_ref[c] for c in range(nslab)], axis=1).astype(BF16)
    qa = qa_ref[...]

    @pl.when(c2 == 0)
    def _():
        m_scr[...] = jnp.full_like(m_scr, NEG)
        l_scr[...] = jnp.zeros_like(l_scr)
        acc_scr[...] = jnp.zeros_like(acc_scr)
        cnew = _pad_rows(cn_ref[...], LANES)
        s = _dot_nt(ql, cnew) + _dot_nt(qa, _pad_rows(krn_ref[...], LANES))
        t_id = lax.broadcasted_iota(jnp.int32, (ns, nh, LANES), 0).reshape(rows, LANES)
        s_id = lax.broadcasted_iota(jnp.int32, (rows, LANES), 1)
        _softmax_step(jnp.where(s_id <= t_id, s, NEG), cnew, m_scr, l_scr, acc_scr, 0)

    def on_chunk(slot):
        ckv = jnp.concatenate([cbuf[slot, p] for p in range(npg)], axis=0).astype(BF16)
        krt = jnp.concatenate([rbuf[slot, p] for p in range(npg)], axis=1).astype(BF16)
        s = _dot_nt(ql, ckv) + _dot(qa[:, :rope], krt)
        _softmax_step(s, ckv, m_scr, l_scr, acc_scr, 0)

    _paged_pipeline(tab_ref, (cc_ref, cr_ref), (cbuf, rbuf), sems,
                    layer=layer, npg=npg, nc=nc, reverse=False, on_chunk=on_chunk)

    @pl.when(c2 == pl.num_programs(1) - 1)
    def _():
        o = acc_scr[0] / l_scr[0]
        for c in range(nslab):
            o_ref[c, 0:rows, :] = o[:, c * LANES:(c + 1) * LANES]
            o_ref[c, rows:, :] = jnp.zeros((o_ref.shape[1] - rows, LANES), o_ref.dtype)


def _mla_dec(page_table, qlat, qaux, cnew, krnew, cckv, ckrt, layer, *, nh, ns):
    bd, n_pages = page_table.shape
    npg = min(PAGES_PER_STEP, n_pages)
    nc = n_pages // npg
    assert nc % 2 == 0
    page, kvl = cckv.shape[-2], cckv.shape[-1]
    rope = ckrt.shape[-2]
    nr = SUBLANES
    rows = ns * nh
    row = lambda b, c, tab: (b, 0)
    seq = lambda b, c, tab: (b, 0, 0)
    hbm = pl.BlockSpec(memory_space=pl.ANY)
    nslab = kvl // LANES
    slab = lambda b, c, tab: (0, b, 0, 0)
    in_specs = [pl.BlockSpec((nslab, None, rows, LANES), slab), pl.BlockSpec((None, rows, LANES), seq),
                pl.BlockSpec((nr, kvl), row), pl.BlockSpec((nr, LANES), row), hbm, hbm]
    return pl.pallas_call(
        functools.partial(_mla_dec_kernel, layer=layer, npg=npg, nc=nc, nh=nh, ns=ns, rope=rope),
        grid_spec=pltpu.PrefetchScalarGridSpec(
            num_scalar_prefetch=1, grid=(bd, nc // 2), in_specs=in_specs,
            out_specs=pl.BlockSpec((nslab, None, nr * nh, LANES), slab),
            scratch_shapes=[pltpu.VMEM((2, npg, page, kvl), F32), pltpu.VMEM((2, npg, rope, page), F32),
                            pltpu.SemaphoreType.DMA((2, 2)),
                            pltpu.VMEM((1, rows, 1), F32), pltpu.VMEM((1, rows, 1), F32),
                            pltpu.VMEM((1, rows, kvl), F32)]),
        out_shape=jax.ShapeDtypeStruct((nslab, bd, nr * nh, LANES), F32),
        compiler_params=_cparams("arbitrary", "arbitrary"),
        name="mla_dec",
    )(page_table, qlat, qaux, cnew, krnew, cckv, ckrt)


def _rope_tables(pos, rope):
    half = rope // 2
    inv = 1.0 / (ROPE_THETA ** (jnp.arange(half, dtype=F32) * (2.0 / rope)))
    ang = pos.astype(F32)[:, None] * inv[None, :]
    zeros = jnp.zeros((pos.shape[0], LANES - rope), F32)
    cos = jnp.concatenate([jnp.cos(ang), jnp.cos(ang), zeros], axis=1)
    sin = jnp.concatenate([jnp.sin(ang), jnp.sin(ang), zeros], axis=1)
    return cos, sin


def _rot_cols(w):
    half = w.shape[-1] // 2
    return jnp.concatenate([-w[..., half:], w[..., :half]], axis=-1)


def _pad_last(w, width):
    pad = [(0, 0)] * (w.ndim - 1) + [(0, width - w.shape[-1])]
    return jnp.pad(w, pad)


def kernel(x_prompt, x_sample, cache_fox_k, cache_fox_v, cache_fox_logf, cache_mla_ckv, cache_mla_krope, page_table, norm_fox, w_in_fox, b_f, w_out_fox, norm_mla, w_in_mla, q_norm, w_q_b, kv_norm, w_uk, w_uv, w_out_mla, final_norm):
    bsz, seq, dm = x_prompt.shape
    bd, ns, _ = x_sample.shape
    nf, n_pool, page, nkv, hd = cache_fox_k.shape
    nm = cache_mla_ckv.shape[0]
    nh = b_f.shape[1]
    grp = nh // nkv
    fw, kvw = nh * hd, nkv * hd
    n_pages = page_table.shape[1]
    past = n_pages * page
    _, ql, mh, qk = w_q_b.shape
    kvl, nope = w_uk.shape[1], w_uk.shape[3]
    rope = qk - nope
    vd = w_uv.shape[3]
    assert hd == LANES and nope == LANES and vd == LANES and rope <= LANES and ns <= SUBLANES
    nr = SUBLANES
    mp, ms = bsz * seq, bd * nr

    hp = x_prompt.reshape(mp, dm)
    hs = jnp.pad(x_sample, ((0, 0), (0, nr - ns), (0, 0))).reshape(ms, dm)

    ck = cache_fox_k.reshape(nf, n_pool, page * nkv, hd)
    cv = cache_fox_v.reshape(nf, n_pool, page * nkv, hd)
    clf = jnp.swapaxes(cache_fox_logf, 2, 3)
    ckrt = jnp.swapaxes(cache_mla_krope, 2, 3)

    cos_p, sin_p = _rope_tables(jnp.arange(seq), rope)
    cos_s, sin_s = _rope_tables(past + jnp.arange(nr), rope)
    cos_s, sin_s = jnp.tile(cos_s, (bd, 1)), jnp.tile(sin_s, (bd, 1))
    aux_tables = _fox_aux_tables(nh, nkv)
    fox_scale = hd ** -0.5 * LOG2E
    mla_scale = qk ** -0.5 * LOG2E
    gfin = final_norm.reshape(1, dm)

    outs = {k: [] for k in ("fk_p", "fv_p", "fl_p", "fk_s", "fv_s", "fl_s", "mc_p", "mr_p", "mc_s", "mr_s")}
    depth = nf + nm
    yp = ys = None
    for i in range(depth):
        j = i // 2
        last = i == depth - 1
        if i % 2 == 0:
            w = w_in_fox[j]
            o1, o2, o3, o4 = fw, fw + kvw, fw + 2 * kvw, fw + 2 * kvw + nh
            w_qz = jnp.stack([w[:, :o1], w[:, o4:]]).astype(BF16)
            wk, wv = w[:, o1:o2].astype(BF16), w[:, o2:o3].astype(BF16)
            wf = _pad_last(w[:, o3:o4], LANES).astype(BF16)
            bfp = _pad_last(b_f[j].reshape(1, nh), LANES)
            g = norm_fox[j].reshape(1, dm)
            w_out = w_out_fox[j].astype(BF16)

            qz = _proj_stack(hp, g, w_qz, (fox_scale, 1.0), tm=1024, tn=1024, name="fox_qz")
            k, v, lf, kb, vb = _fox_kvf(hp, g, wk, wv, wf, bfp, nh=nh, tm=512)
            qa, ka = _fox_aux(lf.reshape(bsz, seq, LANES), aux_tables, nh=nh, nkv=nkv, tc=512)
            qz4 = qz.reshape(2, bsz, seq, fw)
            o = _flash(qz4, qa, kb.reshape(bsz, seq, kvw), ka, vb.reshape(bsz, seq, kvw), qz4,
                       nkv=nkv, group=grp, tq=256, tk=256, qm_lead=0, km_lead=None, v_lead=None,
                       z_lead=1, ka_shared=False, name="fox_flash")
            o = o.reshape(mp, fw)
            outs["fk_p"].append(k.reshape(bsz, seq, nkv, hd))
            outs["fv_p"].append(v.reshape(bsz, seq, nkv, hd))
            outs["fl_p"].append(lf[:, :nh].reshape(bsz, seq, nh))
            if last:
                yp = _out_final(o, None, 0, w_out, hp, gfin, tm=256)
            else:
                hp = _out_proj(o, None, 0, w_out, hp, tm=1024, tn=1024)

            qz = _proj_stack(hs, g, w_qz, (fox_scale, 1.0), tm=1024, tn=512, name="fox_qz_s")
            k, v, lf, kb, vb = _fox_kvf(hs, g, wk, wv, wf, bfp, nh=nh, tm=512)
            o = _fox_dec(page_table, qz, 0, lf, kb, vb, ck, cv, clf, j, nh=nh, nkv=nkv)
            outs["fk_s"].append(k.reshape(bd, nr, nkv, hd)[:, :ns])
            outs["fv_s"].append(v.reshape(bd, nr, nkv, hd)[:, :ns])
            outs["fl_s"].append(lf[:, :nh].reshape(bd, nr, nh)[:, :ns])
            if last:
                ys = _out_final(o, qz, 1, w_out, hs, gfin, tm=256)
            else:
                hs = _out_proj(o, qz, 1, w_out, hs, tm=512, tn=512)
        else:
            w = w_in_mla[j]
            o1, o2, o3 = ql, ql + kvl, ql + kvl + rope
            wqa, wkva = w[:, :o1].astype(BF16), w[:, o1:o2].astype(BF16)
            wkr = w[:, o2:o3]
            wkr2 = jnp.concatenate([_pad_last(wkr, LANES), _pad_last(_rot_cols(wkr), LANES)], axis=1).astype(BF16)
            wz = w[:, o3:].astype(BF16)[None]
            g = norm_mla[j].reshape(1, dm)
            qn, kvn = q_norm[j].reshape(1, ql), kv_norm[j].reshape(1, kvl)
            wqb = w_q_b[j]
            wn = wqb[:, :, :nope].reshape(ql, mh * nope).astype(BF16)
            wr = _pad_last(wqb[:, :, nope:], LANES).reshape(ql, mh * LANES).astype(BF16)
            wrr = _pad_last(_rot_cols(wqb[:, :, nope:]), LANES).reshape(ql, mh * LANES).astype(BF16)
            w_kv = jnp.stack([w_uk[j].reshape(kvl, mh * nope), w_uv[j].reshape(kvl, mh * vd)]).astype(BF16)
            w_ukt = jnp.transpose(w_uk[j], (1, 2, 0)).astype(BF16)
            w_uvh = jnp.transpose(w_uv[j], (1, 0, 2)).astype(BF16)
            w_out = w_out_mla[j].astype(BF16)

            z = _proj_stack(hp, g, wz, (1.0,), tm=1024, tn=1024, name="mla_z")
            qan, ckv, ckvb, kr, krp = _mla_a(hp, g, wqa, wkva, wkr2, qn, kvn, cos_p, sin_p, rope=rope, tm=512)
            qm, qx = _mla_q(qan, wn, wr, wrr, cos_p, sin_p, scale=mla_scale, tm=1024, hb=4)
            kvu = _proj_stack(ckvb, None, w_kv, (1.0, 1.0), tm=1024, tn=1024, name="mla_kv")
            o = _flash(qm.reshape(bsz, seq, mh * nope), qx.reshape(bsz, seq, mh * LANES),
                       kvu.reshape(2, bsz, seq, mh * nope), krp.reshape(bsz, seq, LANES),
                       kvu.reshape(2, bsz, seq, mh * vd), z.reshape(1, bsz, seq, mh * vd),
                       nkv=mh, group=1, tq=512, tk=256, qm_lead=None, km_lead=0, v_lead=1,
                       z_lead=0, ka_shared=True, name="mla_flash")
            o = o.reshape(mp, mh * vd)
            outs["mc_p"].append(ckv.reshape(bsz, seq, kvl))
            outs["mr_p"].append(kr.reshape(bsz, seq, rope))
            if last:
                yp = _out_final(o, None, 0, w_out, hp, gfin, tm=256)
            else:
                hp = _out_proj(o, None, 0, w_out, hp, tm=1024, tn=1024)

            z = _proj_stack(hs, g, wz, (1.0,), tm=1024, tn=512, name="mla_z_s")
            qan, ckv, ckvb, kr, krp = _mla_a(hs, g, wqa, wkva, wkr2, qn, kvn, cos_s, sin_s, rope=rope, tm=512)
            qm, qx = _mla_q(qan, wn, wr, wrr, cos_s, sin_s, scale=mla_scale, tm=1024, hb=4)
            qlat = _absorb(qm, w_ukt, tm=256)
            qx3 = qx.reshape(bd, nr, mh, LANES)[:, :ns].reshape(bd, ns * mh, LANES)
            olat = _mla_dec(page_table, qlat.reshape(kvl // LANES, bd, nr * mh, LANES), qx3,
                            ckvb, krp, cache_mla_ckv, ckrt, j, nh=mh, ns=ns)
            o = _unabsorb(olat.reshape(kvl // LANES, ms * mh, LANES), w_uvh, tm=128)
            outs["mc_s"].append(ckv.reshape(bd, nr, kvl)[:, :ns])
            outs["mr_s"].append(kr.reshape(bd, nr, rope)[:, :ns])
            if last:
                ys = _out_final(o, z, 0, w_out, hs, gfin, tm=256)
            else:
                hs = _out_proj(o, z, 0, w_out, hs, tm=512, tn=512)

    y_prompt = yp.reshape(bsz, seq, dm)
    y_sample = ys.reshape(bd, nr, dm)[:, :ns]
    st = lambda key: jnp.stack(outs[key])
    return (y_prompt, y_sample, st("fk_p"), st("fv_p"), st("fl_p"), st("fk_s"), st("fv_s"), st("fl_s"),
            st("mc_p"), st("mr_p"), st("mc_s"), st("mr_s"))
```

```python
import functools

import numpy as np
import jax
import jax.numpy as jnp
from jax import lax
from jax.experimental import pallas as pl
from jax.experimental.pallas import tpu as pltpu

F32 = jnp.float32
BF16 = jnp.bfloat16
EPS = 1e-6
ROPE_THETA = 10000.0
LOG2E = 1.4426950408889634
NEG = -1e30
LANES = 128
SUBLANES = 8
VMEM_LIMIT = 52 * 1024 * 1024
PAGES_PER_STEP = 8
PAGE_SLOTS = 4


def _cparams(*sem):
    return pltpu.CompilerParams(dimension_semantics=sem, vmem_limit_bytes=VMEM_LIMIT)


def _rms(x, g):
    return x * lax.rsqrt(jnp.mean(x * x, axis=-1, keepdims=True) + EPS) * g


def _dot(a, b):
    return jnp.dot(a, b, preferred_element_type=F32)


def _dot_nt(a, b):
    return lax.dot_general(a, b, (((1,), (1,)), ((), ())), preferred_element_type=F32)


def _split3(x):
    hi = x.astype(BF16)
    r1 = x - hi.astype(F32)
    mid = r1.astype(BF16)
    lo = (r1 - mid.astype(F32)).astype(BF16)
    return hi, mid, lo


def _log_sigmoid(x):
    return jnp.minimum(x, 0.0) - jnp.log1p(jnp.exp(-jnp.abs(x)))


def _proj_kernel(*refs, norm, scales):
    if norm:
        x_ref, g_ref, w_ref, o_ref, h_ref = refs
    else:
        x_ref, w_ref, o_ref = refs
    gi = pl.program_id(1)
    if norm:
        @pl.when((gi == 0) & (pl.program_id(2) == 0))
        def _():
            h_ref[...] = _rms(x_ref[...], g_ref[...]).astype(BF16)
        h = h_ref[...]
    else:
        h = x_ref[...]
    acc = _dot(h, w_ref[...])
    if any(s != 1.0 for s in scales):
        sc = jnp.float32(scales[-1])
        for idx in range(len(scales) - 2, -1, -1):
            sc = jnp.where(gi == idx, jnp.float32(scales[idx]), sc)
        acc = acc * sc
    o_ref[...] = acc.astype(o_ref.dtype)


def _proj_stack(x, g, w, scales, *, tm, tn, name):
    m, kd = x.shape
    ng, _, n = w.shape
    norm = g is not None
    tm, tn = min(tm, m), min(tn, n)
    assert m % tm == 0 and n % tn == 0
    in_specs = [pl.BlockSpec((tm, kd), lambda i, k, j: (i, 0))]
    args = [x]
    scratch = []
    if norm:
        in_specs.append(pl.BlockSpec((1, kd), lambda i, k, j: (0, 0)))
        args.append(g)
        scratch.append(pltpu.VMEM((tm, kd), BF16))
    in_specs.append(pl.BlockSpec((None, kd, tn), lambda i, k, j: (k, 0, j)))
    args.append(w)
    return pl.pallas_call(
        functools.partial(_proj_kernel, norm=norm, scales=tuple(scales)),
        grid=(m // tm, ng, n // tn),
        in_specs=in_specs,
        out_specs=pl.BlockSpec((None, tm, tn), lambda i, k, j: (k, i, j)),
        out_shape=jax.ShapeDtypeStruct((ng, m, n), BF16),
        scratch_shapes=scratch,
        compiler_params=_cparams("parallel", "arbitrary", "arbitrary"),
        name=name,
    )(*args)


def _fox_kvf_kernel(x_ref, g_ref, wk_ref, wv_ref, wf_ref, bf_ref,
                    k_ref, v_ref, lf_ref, kb_ref, vb_ref, *, nh):
    h = _rms(x_ref[...], g_ref[...]).astype(BF16)
    k = _dot(h, wk_ref[...])
    v = _dot(h, wv_ref[...])
    tm = k.shape[0]
    nkv = k.shape[1] // LANES
    for g in range(nkv):
        k_ref[pl.ds(g, tm, stride=nkv), :] = k[:, g * LANES:(g + 1) * LANES]
        v_ref[pl.ds(g, tm, stride=nkv), :] = v[:, g * LANES:(g + 1) * LANES]
    kb_ref[...] = k.astype(BF16)
    vb_ref[...] = v.astype(BF16)
    lf = _log_sigmoid(_dot(h, wf_ref[...]) + bf_ref[...])
    lane = lax.broadcasted_iota(jnp.int32, lf.shape, 1)
    lf_ref[...] = jnp.where(lane < nh, lf, 0.0)


def _fox_kvf(x, g, wk, wv, wf, bf, *, nh, tm):
    m, kd = x.shape
    kvw = wk.shape[1]
    nkv = kvw // LANES
    tm = min(tm, m)
    assert m % tm == 0
    row = lambda i: (i, 0)
    fix = lambda i: (0, 0)
    return pl.pallas_call(
        functools.partial(_fox_kvf_kernel, nh=nh),
        grid=(m // tm,),
        in_specs=[pl.BlockSpec((tm, kd), row), pl.BlockSpec((1, kd), fix),
                  pl.BlockSpec((kd, kvw), fix), pl.BlockSpec((kd, kvw), fix),
                  pl.BlockSpec((kd, LANES), fix), pl.BlockSpec((1, LANES), fix)],
        out_specs=[pl.BlockSpec((tm * nkv, LANES), row), pl.BlockSpec((tm * nkv, LANES), row),
                   pl.BlockSpec((tm, LANES), row), pl.BlockSpec((tm, kvw), row),
                   pl.BlockSpec((tm, kvw), row)],
        out_shape=[jax.ShapeDtypeStruct((m * nkv, LANES), F32), jax.ShapeDtypeStruct((m * nkv, LANES), F32),
                   jax.ShapeDtypeStruct((m, LANES), F32), jax.ShapeDtypeStruct((m, kvw), BF16),
                   jax.ShapeDtypeStruct((m, kvw), BF16)],
        compiler_params=_cparams("parallel"),
        name="fox_kvf",
    )(x, g, wk, wv, wf, bf)


def _fox_aux_kernel(lf_ref, sq_ref, qc_ref, sk_ref, kc_ref, qa_ref, ka_ref, carry_ref):
    @pl.when(pl.program_id(1) == 0)
    def _():
        carry_ref[...] = jnp.zeros_like(carry_ref)
    lf = lf_ref[...]
    tc = lf.shape[0]
    r = lax.broadcasted_iota(jnp.int32, (tc, tc), 0)
    c = lax.broadcasted_iota(jnp.int32, (tc, tc), 1)
    tri = jnp.where(r >= c, 1.0, 0.0).astype(BF16)
    p = _split3(lf)
    cc = carry_ref[...] + (_dot(tri, p[0]) + _dot(tri, p[1]) + _dot(tri, p[2]))
    carry_ref[...] = cc[tc - 1:tc, :]
    p2 = _split3(cc * LOG2E)
    qa = qc_ref[...] + (_dot(p2[0], sq_ref[0]) + _dot(p2[1], sq_ref[1]) + _dot(p2[2], sq_ref[2]))
    ka = kc_ref[...] + (_dot(p2[0], sk_ref[0]) + _dot(p2[1], sk_ref[1]) + _dot(p2[2], sk_ref[2]))
    qa_ref[...] = qa.astype(BF16)
    ka_ref[...] = ka.astype(BF16)


def _fox_aux_tables(nh, nkv):
    grp = nh // nkv
    sq = np.zeros((3, LANES, nh * LANES), np.float32)
    qc = np.zeros((1, nh * LANES), np.float32)
    sk = np.zeros((3, LANES, nkv * LANES), np.float32)
    kc = np.zeros((1, nkv * LANES), np.float32)
    for h in range(nh):
        g, j = divmod(h, grp)
        for piece in range(3):
            sq[piece, h, h * LANES + piece] = 1.0
            qc[0, h * LANES + 3 + 3 * j + piece] = 1.0
            sk[piece, h, g * LANES + 3 + 3 * j + piece] = -1.0
    for g in range(nkv):
        kc[0, g * LANES:g * LANES + 3] = 1.0
    return (jnp.asarray(sq, BF16), jnp.asarray(qc, F32), jnp.asarray(sk, BF16), jnp.asarray(kc, F32))


def _fox_aux(lf, tables, *, nh, nkv, tc):
    b, s, _ = lf.shape
    sq, qc, sk, kc = tables
    tc = min(tc, s)
    assert s % tc == 0
    return pl.pallas_call(
        _fox_aux_kernel,
        grid=(b, s // tc),
        in_specs=[pl.BlockSpec((None, tc, LANES), lambda bi, i: (bi, i, 0)),
                  pl.BlockSpec(sq.shape, lambda bi, i: (0, 0, 0)),
                  pl.BlockSpec(qc.shape, lambda bi, i: (0, 0)),
                  pl.BlockSpec(sk.shape, lambda bi, i: (0, 0, 0)),
                  pl.BlockSpec(kc.shape, lambda bi, i: (0, 0))],
        out_specs=[pl.BlockSpec((None, tc, nh * LANES), lambda bi, i: (bi, i, 0)),
                   pl.BlockSpec((None, tc, nkv * LANES), lambda bi, i: (bi, i, 0))],
        out_shape=[jax.ShapeDtypeStruct((b, s, nh * LANES), BF16),
                   jax.ShapeDtypeStruct((b, s, nkv * LANES), BF16)],
        scratch_shapes=[pltpu.VMEM((1, LANES), F32)],
        compiler_params=_cparams("parallel", "arbitrary"),
        name="fox_aux",
    )(lf, sq, qc, sk, kc)


def _silu(z):
    return z * (1.0 / (1.0 + jnp.exp(-z)))


def _flash_kernel(qm_ref, qa_ref, km_ref, ka_ref, v_ref, z_ref, o_ref,
                  q_scr, s0_scr, s1_scr, m_scr, acc_scr, *, group, tq, tk, hd):
    i = pl.program_id(2)
    rows = group * tq
    per = tq // tk
    s_bufs = (s0_scr, s1_scr)
    for j in range(group):
        q_scr[j * tq:(j + 1) * tq, 0:hd] = qm_ref[:, j * hd:(j + 1) * hd]
        q_scr[j * tq:(j + 1) * tq, hd:2 * hd] = qa_ref[:, j * hd:(j + 1) * hd]
    m_scr[...] = jnp.full_like(m_scr, NEG)
    acc_scr[...] = jnp.zeros_like(acc_scr)
    ones = jnp.ones((tk, hd), BF16)

    def scores(step, slot, mask_off):
        ks = pl.multiple_of(step * tk, tk)
        k = jnp.concatenate([km_ref[pl.ds(ks, tk), :], ka_ref[pl.ds(ks, tk), :]], axis=1)
        s = _dot_nt(q_scr[...], k)
        if mask_off is not None:
            t = lax.broadcasted_iota(jnp.int32, (rows, tk), 0) & (tq - 1)
            col = lax.broadcasted_iota(jnp.int32, (rows, tk), 1) + mask_off
            s = jnp.where(col <= t, s, NEG)
        s_bufs[slot][...] = s

    def consume(step, slot):
        ks = pl.multiple_of(step * tk, tk)
        s_ref = s_bufs[slot]
        m_prev = m_scr[...]
        m_new = jnp.maximum(m_prev, jnp.max(s_ref[...], axis=1, keepdims=True))
        alpha = jnp.exp2(m_prev - m_new)
        p = jnp.exp2(s_ref[...] - jnp.tile(m_new, (1, tk // hd)))
        v1 = jnp.concatenate([v_ref[pl.ds(ks, tk), :], ones], axis=1)
        acc_scr[...] = jnp.tile(alpha, (1, 2)) * acc_scr[...] + _dot(p.astype(BF16), v1)
        m_scr[...] = m_new

    def tail(first, n_plain):
        seq = [(first + n, None) for n in range(n_plain)]
        seq += [(first + n_plain + d, d * tk) for d in range(per)]
        for n, (step, _) in enumerate(seq):
            if n + 1 < len(seq):
                scores(seq[n + 1][0], (n + 1) % 2, seq[n + 1][1])
            consume(step, n % 2)

    n_plain = i * per

    @pl.when(i == 0)
    def _():
        scores(0, 0, 0)
        tail(0, 0)

    @pl.when(i > 0)
    def _():
        scores(0, 0, None)
        n_pairs = (n_plain - 1) // 2

        def body(jj, carry):
            scores(2 * jj + 1, 1, None)
            consume(2 * jj, 0)
            scores(2 * jj + 2, 0, None)
            consume(2 * jj + 1, 1)
            return carry

        lax.fori_loop(0, n_pairs, body, 0)
        if per % 2 == 0:
            tail(2 * n_pairs, 2)
        else:
            @pl.when(n_plain % 2 == 1)
            def _():
                tail(2 * n_pairs, 1)

            @pl.when(n_plain % 2 == 0)
            def _():
                tail(2 * n_pairs, 2)

    acc = acc_scr[...]
    o = acc[:, :hd] / acc[:, hd:]
    for j in range(group):
        gate = _silu(z_ref[:, j * hd:(j + 1) * hd].astype(F32))
        o_ref[:, j * hd:(j + 1) * hd] = (o[j * tq:(j + 1) * tq, :] * gate).astype(o_ref.dtype)


def _flash(qm, qa, km, ka, v, z, *, nkv, group, tq, tk, qm_lead, km_lead, v_lead, z_lead, ka_shared, name):
    hd = LANES
    b, s = qa.shape[0], qa.shape[1]
    tq, tk = min(tq, s), min(tk, s)
    assert s % tq == 0 and tq % tk == 0

    def spec(lead, blk_rows, width, imap):
        if lead is None:
            return pl.BlockSpec((None, blk_rows, width), imap)
        return pl.BlockSpec((None, None, blk_rows, width), lambda bi, g, i: (lead,) + imap(bi, g, i))

    q_map = lambda bi, g, i: (bi, i, g)
    k_map = lambda bi, g, i: (bi, 0, g)
    ka_map = (lambda bi, g, i: (bi, 0, 0)) if ka_shared else k_map
    return pl.pallas_call(
        functools.partial(_flash_kernel, group=group, tq=tq, tk=tk, hd=hd),
        grid=(b, nkv, s // tq),
        in_specs=[spec(qm_lead, tq, group * hd, q_map), spec(None, tq, group * hd, q_map),
                  spec(km_lead, s, hd, k_map), spec(None, s, hd, ka_map), spec(v_lead, s, hd, k_map),
                  spec(z_lead, tq, group * hd, q_map)],
        out_specs=pl.BlockSpec((None, tq, group * hd), q_map),
        out_shape=jax.ShapeDtypeStruct((b, s, nkv * group * hd), BF16),
        scratch_shapes=[pltpu.VMEM((group * tq, 2 * hd), BF16), pltpu.VMEM((group * tq, tk), F32),
                        pltpu.VMEM((group * tq, tk), F32), pltpu.VMEM((group * tq, hd), F32),
                        pltpu.VMEM((group * tq, 2 * hd), F32)],
        compiler_params=_cparams("parallel", "parallel", "arbitrary"),
        name=name,
    )(qm, qa, km, ka, v, z)


def _gated(o_ref, z_ref):
    if z_ref is None:
        return o_ref[...]
    return (o_ref[...].astype(F32) * _silu(z_ref[...].astype(F32))).astype(BF16)


def _out_proj_kernel(*refs, gated):
    if gated:
        o_ref, z_ref, w_ref, r_ref, y_ref, u_scr = refs

        @pl.when(pl.program_id(1) == 0)
        def _():
            u_scr[...] = _gated(o_ref, z_ref)
        u = u_scr[...]
    else:
        o_ref, w_ref, r_ref, y_ref = refs
        u = o_ref[...]
    y_ref[...] = r_ref[...] + _dot(u, w_ref[...])


def _out_proj(o, z, z_lead, w, resid, *, tm, tn):
    m, kd = o.shape
    n = w.shape[1]
    tm, tn = min(tm, m), min(tn, n)
    assert m % tm == 0 and n % tn == 0
    gated = z is not None
    in_specs = [pl.BlockSpec((tm, kd), lambda i, j: (i, 0))]
    args = [o]
    if gated:
        in_specs.append(pl.BlockSpec((None, tm, kd), lambda i, j: (z_lead, i, 0)))
        args.append(z)
    in_specs += [pl.BlockSpec((kd, tn), lambda i, j: (0, j)), pl.BlockSpec((tm, tn), lambda i, j: (i, j))]
    return pl.pallas_call(
        functools.partial(_out_proj_kernel, gated=gated),
        grid=(m // tm, n // tn),
        in_specs=in_specs,
        out_specs=pl.BlockSpec((tm, tn), lambda i, j: (i, j)),
        out_shape=jax.ShapeDtypeStruct((m, n), F32),
        scratch_shapes=[pltpu.VMEM((tm, kd), BF16)] if gated else [],
        compiler_params=_cparams("parallel", "arbitrary"),
        name="out_proj",
    )(*args, w, resid)


def _out_final_kernel(*refs, gated):
    if gated:
        o_ref, z_ref, w_ref, r_ref, g_ref, y_ref = refs
    else:
        o_ref, w_ref, r_ref, g_ref, y_ref = refs
        z_ref = None
    hsum = r_ref[...] + _dot(_gated(o_ref, z_ref), w_ref[...])
    y_ref[...] = _rms(hsum, g_ref[...])


def _out_final(o, z, z_lead, w, resid, gfin, *, tm):
    m, kd = o.shape
    n = w.shape[1]
    tm = min(tm, m)
    assert m % tm == 0
    gated = z is not None
    in_specs = [pl.BlockSpec((tm, kd), lambda i: (i, 0))]
    args = [o]
    if gated:
        in_specs.append(pl.BlockSpec((None, tm, kd), lambda i: (z_lead, i, 0)))
        args.append(z)
    in_specs += [pl.BlockSpec((kd, n), lambda i: (0, 0)), pl.BlockSpec((tm, n), lambda i: (i, 0)),
                 pl.BlockSpec((1, n), lambda i: (0, 0))]
    return pl.pallas_call(
        functools.partial(_out_final_kernel, gated=gated),
        grid=(m // tm,),
        in_specs=in_specs,
        out_specs=pl.BlockSpec((tm, n), lambda i: (i, 0)),
        out_shape=jax.ShapeDtypeStruct((m, n), F32),
        compiler_params=_cparams("parallel"),
        name="out_final",
    )(*args, w, resid, gfin)


def _mla_a_kernel(x_ref, g_ref, wqa_ref, wkva_ref, wkr_ref, qn_ref, kvn_ref, cos_ref, sin_ref,
                  qa_ref, ckv_ref, ckvb_ref, kr_ref, krp_ref, *, rope):
    h = _rms(x_ref[...], g_ref[...]).astype(BF16)
    qa_ref[...] = _rms(_dot(h, wqa_ref[...]), qn_ref[...]).astype(BF16)
    ckv = _rms(_dot(h, wkva_ref[...]), kvn_ref[...])
    ckv_ref[...] = ckv
    ckvb_ref[...] = ckv.astype(BF16)
    kr2 = _dot(h, wkr_ref[...])
    krp = kr2[:, :LANES] * cos_ref[...] + kr2[:, LANES:] * sin_ref[...]
    kr_ref[...] = krp[:, :rope]
    krp_ref[...] = krp.astype(BF16)


def _mla_a(x, g, wqa, wkva, wkr2, qn, kvn, cos, sin, *, rope, tm):
    m, kd = x.shape
    ql, kvl = wqa.shape[1], wkva.shape[1]
    tm = min(tm, m)
    assert m % tm == 0
    nt = cos.shape[0] // tm
    row = lambda i: (i, 0)
    fix = lambda i: (0, 0)
    tab = lambda i: (i % nt, 0)
    return pl.pallas_call(
        functools.partial(_mla_a_kernel, rope=rope),
        grid=(m // tm,),
        in_specs=[pl.BlockSpec((tm, kd), row), pl.BlockSpec((1, kd), fix),
                  pl.BlockSpec((kd, ql), fix), pl.BlockSpec((kd, kvl), fix),
                  pl.BlockSpec((kd, 2 * LANES), fix), pl.BlockSpec((1, ql), fix),
                  pl.BlockSpec((1, kvl), fix), pl.BlockSpec((tm, LANES), tab),
                  pl.BlockSpec((tm, LANES), tab)],
        out_specs=[pl.BlockSpec((tm, ql), row), pl.BlockSpec((tm, kvl), row),
                   pl.BlockSpec((tm, kvl), row), pl.BlockSpec((tm, rope), row),
                   pl.BlockSpec((tm, LANES), row)],
        out_shape=[jax.ShapeDtypeStruct((m, ql), BF16), jax.ShapeDtypeStruct((m, kvl), F32),
                   jax.ShapeDtypeStruct((m, kvl), BF16), jax.ShapeDtypeStruct((m, rope), F32),
                   jax.ShapeDtypeStruct((m, LANES), BF16)],
        compiler_params=_cparams("parallel"),
        name="mla_a",
    )(x, g, wqa, wkva, wkr2, qn, kvn, cos, sin)


def _mla_q_kernel(x_ref, wn_ref, wr_ref, wrr_ref, cos_ref, sin_ref, qm_ref, qa_ref, *, hb, scale):
    x = x_ref[...]
    qm_ref[...] = (_dot(x, wn_ref[...]) * scale).astype(BF16)
    qr = _dot(x, wr_ref[...])
    qrr = _dot(x, wrr_ref[...])
    cos = cos_ref[...]
    sin = sin_ref[...]
    for jj in range(hb):
        sl = slice(jj * LANES, (jj + 1) * LANES)
        qa_ref[:, sl] = ((qr[:, sl] * cos + qrr[:, sl] * sin) * scale).astype(BF16)


def _mla_q(x, wn, wr, wrr, cos, sin, *, scale, tm, hb):
    m, kd = x.shape
    n = wn.shape[1]
    tm = min(tm, m)
    assert m % tm == 0
    tn = hb * LANES
    nt = cos.shape[0] // tm
    wmap = lambda i, j: (0, j)
    omap = lambda i, j: (i, j)
    tab = lambda i, j: (i % nt, 0)
    return pl.pallas_call(
        functools.partial(_mla_q_kernel, hb=hb, scale=scale),
        grid=(m // tm, n // tn),
        in_specs=[pl.BlockSpec((tm, kd), lambda i, j: (i, 0)),
                  pl.BlockSpec((kd, tn), wmap), pl.BlockSpec((kd, tn), wmap), pl.BlockSpec((kd, tn), wmap),
                  pl.BlockSpec((tm, LANES), tab), pl.BlockSpec((tm, LANES), tab)],
        out_specs=[pl.BlockSpec((tm, tn), omap), pl.BlockSpec((tm, tn), omap)],
        out_shape=[jax.ShapeDtypeStruct((m, n), BF16), jax.ShapeDtypeStruct((m, n), BF16)],
        compiler_params=_cparams("parallel", "arbitrary"),
        name="mla_q",
    )(x, wn, wr, wrr, cos, sin)


def _absorb_kernel(x_ref, w_ref, o_ref, *, nh):
    tm = x_ref.shape[0]
    kd = x_ref.shape[1] // nh
    for h in range(nh):
        y = _dot(x_ref[:, h * kd:(h + 1) * kd], w_ref[h])
        for c in range(o_ref.shape[0]):
            o_ref[c, pl.ds(h, tm, stride=nh), :] = y[:, c * LANES:(c + 1) * LANES]


def _absorb(x, w, *, tm):
    m = x.shape[0]
    nh, kd, n = w.shape
    tm = min(tm, m)
    assert m % tm == 0
    return pl.pallas_call(
        functools.partial(_absorb_kernel, nh=nh),
        grid=(m // tm,),
        in_specs=[pl.BlockSpec((tm, nh * kd), lambda i: (i, 0)), pl.BlockSpec((nh, kd, n), lambda i: (0, 0, 0))],
        out_specs=pl.BlockSpec((n // LANES, tm * nh, LANES), lambda i: (0, i, 0)),
        out_shape=jax.ShapeDtypeStruct((n // LANES, m * nh, LANES), F32),
        compiler_params=_cparams("parallel"),
        name="mla_absorb",
    )(x, w)


def _unabsorb_kernel(x_ref, w_ref, o_ref, *, nh):
    tm = o_ref.shape[0]
    n = o_ref.shape[1] // nh
    for h in range(nh):
        xh = jnp.concatenate([x_ref[c, pl.ds(h, tm, stride=nh), :] for c in range(x_ref.shape[0])], axis=1)
        o_ref[:, h * n:(h + 1) * n] = _dot(xh.astype(BF16), w_ref[h]).astype(o_ref.dtype)


def _unabsorb(x, w, *, tm):
    nh, kd, n = w.shape
    m = x.shape[1] // nh
    tm = min(tm, m)
    assert m % tm == 0
    return pl.pallas_call(
        functools.partial(_unabsorb_kernel, nh=nh),
        grid=(m // tm,),
        in_specs=[pl.BlockSpec((kd // LANES, tm * nh, LANES), lambda i: (0, i, 0)),
                  pl.BlockSpec((nh, kd, n), lambda i: (0, 0, 0))],
        out_specs=pl.BlockSpec((tm, nh * n), lambda i: (i, 0)),
        out_shape=jax.ShapeDtypeStruct((m, nh * n), BF16),
        compiler_params=_cparams("parallel"),
        name="mla_uv",
    )(x, w)


def _softmax_step(s, v, m_scr, l_scr, acc_scr, idx):
    m_prev = m_scr[idx]
    m_new = jnp.maximum(m_prev, jnp.max(s, axis=1, keepdims=True))
    alpha = jnp.exp2(m_prev - m_new)
    p = jnp.exp2(s - m_new)
    l_scr[idx] = alpha * l_scr[idx] + jnp.sum(p, axis=1, keepdims=True)
    acc_scr[idx] = alpha * acc_scr[idx] + _dot(p.astype(BF16), v)
    m_scr[idx] = m_new


def _pad_rows(x, rows):
    return jnp.concatenate([x, jnp.zeros((rows - x.shape[0], x.shape[1]), x.dtype)], axis=0)


def _page_copies(tab_ref, seq, first_page, layer, srcs, bufs, sems, slot, npg):
    copies = []
    for p in range(npg):
        pg = tab_ref[seq, first_page + p]
        for a, (src, buf) in enumerate(zip(srcs, bufs)):
            copies.append(pltpu.make_async_copy(src.at[layer, pg], buf.at[slot, p], sems.at[a, slot]))
    return copies


def _paged_pipeline(tab_ref, srcs, bufs, sems, *, layer, npg, nc, reverse, on_chunk):
    b, c2 = pl.program_id(0), pl.program_id(1)
    total = pl.num_programs(0) * nc
    n0 = (b * (nc // PAGE_SLOTS) + c2) * PAGE_SLOTS

    def copies(n, slot):
        seq, cc = n // nc, n % nc
        first = ((nc - 1 - cc) if reverse else cc) * npg
        return _page_copies(tab_ref, seq, first, layer, srcs, bufs, sems, slot, npg)

    @pl.when(n0 == 0)
    def _():
        for slot in range(PAGE_SLOTS):
            for cp in copies(slot, slot):
                cp.start()

    for slot in range(PAGE_SLOTS):
        n = n0 + slot
        for cp in copies(n, slot):
            cp.wait()
        on_chunk(slot)

        @pl.when(n + PAGE_SLOTS < total)
        def _():
            for cp in copies(n + PAGE_SLOTS, slot):
                cp.start()


def _fox_dec_kernel(tab_ref, q_ref, lfn_ref, kn_ref, vn_ref, ck_ref, cv_ref, clf_ref, o_ref,
                    kbuf, vbuf, lbuf, sems, m_scr, l_scr, acc_scr, car_scr, cn_scr,
                    *, layer, npg, nc, nkv, grp, page):
    c2 = pl.program_id(1)
    hd = LANES
    nr = SUBLANES
    grows = grp * nr

    def q_pair(pr):
        blocks = []
        for gg in range(2):
            g = 2 * pr + gg
            qg = jnp.concatenate([q_ref[:, (g * grp + j) * hd:(g * grp + j + 1) * hd] for j in range(grp)], axis=0)
            zero = jnp.zeros_like(qg)
            blocks.append(jnp.concatenate([qg, zero] if gg == 0 else [zero, qg], axis=1))
        return jnp.concatenate(blocks, axis=0)

    def biased(s, pr, col_bias, row_bias, sign, mask):
        parts = []
        for gg in range(2):
            for j in range(grp):
                h = (2 * pr + gg) * grp + j
                r0 = gg * grows + j * nr
                sj = s[r0:r0 + nr, :] + col_bias[:, h:h + 1] + sign * row_bias[h:h + 1, :]
                parts.append(sj if mask is None else jnp.where(mask, sj, NEG))
        return jnp.concatenate(parts, axis=0)

    @pl.when(c2 == 0)
    def _():
        m_scr[...] = jnp.full_like(m_scr, NEG)
        l_scr[...] = jnp.zeros_like(l_scr)
        acc_scr[...] = jnp.zeros_like(acc_scr)
        car_scr[...] = jnp.zeros_like(car_scr)
        lf = lfn_ref[...]
        rid = lax.broadcasted_iota(jnp.int32, lf.shape, 0)
        cn = jnp.zeros_like(lf)
        for t in range(nr):
            cn = cn + jnp.where(rid >= t, lf[t:t + 1, :], 0.0)
        cn = cn * LOG2E
        cn_scr[...] = cn
        cnt = _pad_rows(cn, LANES).T
        mask = (lax.broadcasted_iota(jnp.int32, (nr, LANES), 1)
                <= lax.broadcasted_iota(jnp.int32, (nr, LANES), 0))
        for pr in range(nkv // 2):
            kk = _pad_rows(kn_ref[:, 2 * pr * hd:(2 * pr + 2) * hd], LANES)
            vv = _pad_rows(vn_ref[:, 2 * pr * hd:(2 * pr + 2) * hd], LANES)
            s = biased(_dot_nt(q_pair(pr), kk), pr, cn, cnt, -1.0, mask)
            _softmax_step(s, vv, m_scr, l_scr, acc_scr, pr)

    def on_chunk(slot):
        lfs = [lbuf[slot, p] for p in range(npg)]
        pieces = [_split3(x) for x in lfs]
        stacked = jnp.concatenate([pieces[p][k] for k in range(3) for p in range(npg)], axis=0)
        r = lax.broadcasted_iota(jnp.int32, (page, page), 0)
        cc = lax.broadcasted_iota(jnp.int32, (page, page), 1)
        upper = jnp.where(r > cc, 1.0, 0.0).astype(BF16)
        res = _dot(stacked, upper)
        nh = lfs[0].shape[0]
        off = car_scr[...]
        dks = [None] * npg
        for p in range(npg - 1, -1, -1):
            loc = (res[p * nh:(p + 1) * nh] + res[(npg + p) * nh:(npg + p + 1) * nh]
                   + res[(2 * npg + p) * nh:(2 * npg + p + 1) * nh])
            dks[p] = (loc + off) * LOG2E
            off = off + (loc[:, 0:1] + lfs[p][:, 0:1])
        car_scr[...] = off
        dk = jnp.concatenate(dks, axis=1)
        cn = cn_scr[...]

        def cached_pair(buf, pr):
            return jnp.concatenate(
                [jnp.concatenate([buf[slot, p, pl.ds(2 * pr + gg, page, stride=nkv), :] for gg in range(2)], axis=1)
                 for p in range(npg)], axis=0).astype(BF16)

        for pr in range(nkv // 2):
            s = biased(_dot_nt(q_pair(pr), cached_pair(kbuf, pr)), pr, cn, dk, 1.0, None)
            _softmax_step(s, cached_pair(vbuf, pr), m_scr, l_scr, acc_scr, pr)

    _paged_pipeline(tab_ref, (ck_ref, cv_ref, clf_ref), (kbuf, vbuf, lbuf), sems,
                    layer=layer, npg=npg, nc=nc, reverse=True, on_chunk=on_chunk)

    @pl.when(c2 == pl.num_programs(1) - 1)
    def _():
        for pr in range(nkv // 2):
            o = acc_scr[pr] / l_scr[pr]
            for gg in range(2):
                for j in range(grp):
                    h = (2 * pr + gg) * grp + j
                    r0 = gg * grows + j * nr
                    o_ref[:, h * hd:(h + 1) * hd] = o[r0:r0 + nr, gg * hd:(gg + 1) * hd].astype(o_ref.dtype)


def _fox_dec(page_table, q, q_lead, lfn, kn, vn, ck, cv, clf, layer, *, nh, nkv):
    bd, n_pages = page_table.shape
    npg = min(PAGES_PER_STEP, n_pages)
    nc = n_pages // npg
    assert nc % PAGE_SLOTS == 0 and nkv % 2 == 0
    page = clf.shape[-1]
    hd = LANES
    nr = SUBLANES
    grp = nh // nkv
    row = lambda b, c, tab: (b, 0)
    hbm = pl.BlockSpec(memory_space=pl.ANY)
    in_specs = [pl.BlockSpec((None, nr, nh * hd), lambda b, c, tab: (q_lead, b, 0)),
                pl.BlockSpec((nr, LANES), row), pl.BlockSpec((nr, nkv * hd), row),
                pl.BlockSpec((nr, nkv * hd), row), hbm, hbm, hbm]
    rows = 2 * grp * nr
    return pl.pallas_call(
        functools.partial(_fox_dec_kernel, layer=layer, npg=npg, nc=nc, nkv=nkv, grp=grp, page=page),
        grid_spec=pltpu.PrefetchScalarGridSpec(
            num_scalar_prefetch=1, grid=(bd, nc // PAGE_SLOTS), in_specs=in_specs,
            out_specs=pl.BlockSpec((nr, nh * hd), row),
            scratch_shapes=[pltpu.VMEM((PAGE_SLOTS, npg, page * nkv, hd), F32), pltpu.VMEM((PAGE_SLOTS, npg, page * nkv, hd), F32),
                            pltpu.VMEM((PAGE_SLOTS, npg, nh, page), F32), pltpu.SemaphoreType.DMA((3, PAGE_SLOTS)),
                            pltpu.VMEM((nkv // 2, rows, 1), F32), pltpu.VMEM((nkv // 2, rows, 1), F32),
                            pltpu.VMEM((nkv // 2, rows, 2 * hd), F32), pltpu.VMEM((nh, 1), F32),
                            pltpu.VMEM((nr, LANES), F32)]),
        out_shape=jax.ShapeDtypeStruct((bd * nr, nh * hd), BF16),
        compiler_params=_cparams("arbitrary", "arbitrary"),
        name="fox_dec",
    )(page_table, q, lfn, kn, vn, ck, cv, clf)


def _mla_dec_kernel(tab_ref, ql_ref, qa_ref, cn_ref, krn_ref, cc_ref, cr_ref, o_ref,
                    cbuf, rbuf, sems, m_scr, l_scr, acc_scr, *, layer, npg, nc, nh, ns, rope):
    c2 = pl.program_id(1)
    rows = ns * nh
    nslab = ql_ref.shape[0]
    ql = jnp.concatenate([ql_ref[c] for c in range(nslab)], axis=1).astype(BF16)
    qa = qa_ref[...]

    @pl.when(c2 == 0)
    def _():
        m_scr[...] = jnp.full_like(m_scr, NEG)
        l_scr[...] = jnp.zeros_like(l_scr)
        acc_scr[...] = jnp.zeros_like(acc_scr)
        cnew = _pad_rows(cn_ref[...], LANES)
        s = _dot_nt(ql, cnew) + _dot_nt(qa, _pad_rows(krn_ref[...], LANES))
        t_id = lax.broadcasted_iota(jnp.int32, (ns, nh, LANES), 0).reshape(rows, LANES)
        s_id = lax.broadcasted_iota(jnp.int32, (rows, LANES), 1)
        _softmax_step(jnp.where(s_id <= t_id, s, NEG), cnew, m_scr, l_scr, acc_scr, 0)

    def on_chunk(slot):
        ckv = jnp.concatenate([cbuf[slot, p] for p in range(npg)], axis=0).astype(BF16)
        krt = jnp.concatenate([rbuf[slot, p] for p in range(npg)], axis=1).astype(BF16)
        s = _dot_nt(ql, ckv) + _dot(qa[:, :rope], krt)
        _softmax_step(s, ckv, m_scr, l_scr, acc_scr, 0)

    _paged_pipeline(tab_ref, (cc_ref, cr_ref), (cbuf, rbuf), sems,
                    layer=layer, npg=npg, nc=nc, reverse=False, on_chunk=on_chunk)

    @pl.when(c2 == pl.num_programs(1) - 1)
    def _():
        o = acc_scr[0] / l_scr[0]
        for c in range(nslab):
            o_ref[c, 0:rows, :] = o[:, c * LANES:(c + 1) * LANES]
            o_ref[c, rows:, :] = jnp.zeros((o_ref.shape[1] - rows, LANES), o_ref.dtype)


def _mla_dec(page_table, qlat, qaux, cnew, krnew, cckv, ckrt, layer, *, nh, ns):
    bd, n_pages = page_table.shape
    npg = min(PAGES_PER_STEP, n_pages)
    nc = n_pages // npg
    assert nc % PAGE_SLOTS == 0
    page, kvl = cckv.shape[-2], cckv.shape[-1]
    rope = ckrt.shape[-2]
    nr = SUBLANES
    rows = ns * nh
    row = lambda b, c, tab: (b, 0)
    seq = lambda b, c, tab: (b, 0, 0)
    hbm = pl.BlockSpec(memory_space=pl.ANY)
    nslab = kvl // LANES
    slab = lambda b, c, tab: (0, b, 0, 0)
    in_specs = [pl.BlockSpec((nslab, None, rows, LANES), slab), pl.BlockSpec((None, rows, LANES), seq),
                pl.BlockSpec((nr, kvl), row), pl.BlockSpec((nr, LANES), row), hbm, hbm]
    return pl.pallas_call(
        functools.partial(_mla_dec_kernel, layer=layer, npg=npg, nc=nc, nh=nh, ns=ns, rope=rope),
        grid_spec=pltpu.PrefetchScalarGridSpec(
            num_scalar_prefetch=1, grid=(bd, nc // PAGE_SLOTS), in_specs=in_specs,
            out_specs=pl.BlockSpec((nslab, None, nr * nh, LANES), slab),
            scratch_shapes=[pltpu.VMEM((PAGE_SLOTS, npg, page, kvl), F32), pltpu.VMEM((PAGE_SLOTS, npg, rope, page), F32),
                            pltpu.SemaphoreType.DMA((2, PAGE_SLOTS)),
                            pltpu.VMEM((1, rows, 1), F32), pltpu.VMEM((1, rows, 1), F32),
                            pltpu.VMEM((1, rows, kvl), F32)]),
        out_shape=jax.ShapeDtypeStruct((nslab, bd, nr * nh, LANES), F32),
        compiler_params=_cparams("arbitrary", "arbitrary"),
        name="mla_dec",
    )(page_table, qlat, qaux, cnew, krnew, cckv, ckrt)


def _rope_tables(pos, rope):
    half = rope // 2
    inv = 1.0 / (ROPE_THETA ** (jnp.arange(half, dtype=F32) * (2.0 / rope)))
    ang = pos.astype(F32)[:, None] * inv[None, :]
    zeros = jnp.zeros((pos.shape[0], LANES - rope), F32)
    cos = jnp.concatenate([jnp.cos(ang), jnp.cos(ang), zeros], axis=1)
    sin = jnp.concatenate([jnp.sin(ang), jnp.sin(ang), zeros], axis=1)
    return cos, sin


def _rot_cols(w):
    half = w.shape[-1] // 2
    return jnp.concatenate([-w[..., half:], w[..., :half]], axis=-1)


def _pad_last(w, width):
    pad = [(0, 0)] * (w.ndim - 1) + [(0, width - w.shape[-1])]
    return jnp.pad(w, pad)


def kernel(x_prompt, x_sample, cache_fox_k, cache_fox_v, cache_fox_logf, cache_mla_ckv, cache_mla_krope, page_table, norm_fox, w_in_fox, b_f, w_out_fox, norm_mla, w_in_mla, q_norm, w_q_b, kv_norm, w_uk, w_uv, w_out_mla, final_norm):
    bsz, seq, dm = x_prompt.shape
    bd, ns, _ = x_sample.shape
    nf, n_pool, page, nkv, hd = cache_fox_k.shape
    nm = cache_mla_ckv.shape[0]
    nh = b_f.shape[1]
    grp = nh // nkv
    fw, kvw = nh * hd, nkv * hd
    n_pages = page_table.shape[1]
    past = n_pages * page
    _, ql, mh, qk = w_q_b.shape
    kvl, nope = w_uk.shape[1], w_uk.shape[3]
    rope = qk - nope
    vd = w_uv.shape[3]
    assert hd == LANES and nope == LANES and vd == LANES and rope <= LANES and ns <= SUBLANES
    nr = SUBLANES
    mp, ms = bsz * seq, bd * nr

    hp = x_prompt.reshape(mp, dm)
    hs = jnp.pad(x_sample, ((0, 0), (0, nr - ns), (0, 0))).reshape(ms, dm)

    ck = cache_fox_k.reshape(nf, n_pool, page * nkv, hd)
    cv = cache_fox_v.reshape(nf, n_pool, page * nkv, hd)
    clf = jnp.swapaxes(cache_fox_logf, 2, 3)
    ckrt = jnp.swapaxes(cache_mla_krope, 2, 3)

    cos_p, sin_p = _rope_tables(jnp.arange(seq), rope)
    cos_s, sin_s = _rope_tables(past + jnp.arange(nr), rope)
    cos_s, sin_s = jnp.tile(cos_s, (bd, 1)), jnp.tile(sin_s, (bd, 1))
    aux_tables = _fox_aux_tables(nh, nkv)
    fox_scale = hd ** -0.5 * LOG2E
    mla_scale = qk ** -0.5 * LOG2E
    gfin = final_norm.reshape(1, dm)

    outs = {k: [] for k in ("fk_p", "fv_p", "fl_p", "fk_s", "fv_s", "fl_s", "mc_p", "mr_p", "mc_s", "mr_s")}
    depth = nf + nm
    yp = ys = None
    for i in range(depth):
        j = i // 2
        last = i == depth - 1
        if i % 2 == 0:
            w = w_in_fox[j]
            o1, o2, o3, o4 = fw, fw + kvw, fw + 2 * kvw, fw + 2 * kvw + nh
            w_qz = jnp.stack([w[:, :o1], w[:, o4:]]).astype(BF16)
            wk, wv = w[:, o1:o2].astype(BF16), w[:, o2:o3].astype(BF16)
            wf = _pad_last(w[:, o3:o4], LANES).astype(BF16)
            bfp = _pad_last(b_f[j].reshape(1, nh), LANES)
            g = norm_fox[j].reshape(1, dm)
            w_out = w_out_fox[j].astype(BF16)

            qz = _proj_stack(hp, g, w_qz, (fox_scale, 1.0), tm=1024, tn=1024, name="fox_qz")
            k, v, lf, kb, vb = _fox_kvf(hp, g, wk, wv, wf, bfp, nh=nh, tm=512)
            qa, ka = _fox_aux(lf.reshape(bsz, seq, LANES), aux_tables, nh=nh, nkv=nkv, tc=512)
            qz4 = qz.reshape(2, bsz, seq, fw)
            o = _flash(qz4, qa, kb.reshape(bsz, seq, kvw), ka, vb.reshape(bsz, seq, kvw), qz4,
                       nkv=nkv, group=grp, tq=256, tk=256, qm_lead=0, km_lead=None, v_lead=None,
                       z_lead=1, ka_shared=False, name="fox_flash")
            o = o.reshape(mp, fw)
            outs["fk_p"].append(k.reshape(bsz, seq, nkv, hd))
            outs["fv_p"].append(v.reshape(bsz, seq, nkv, hd))
            outs["fl_p"].append(lf[:, :nh].reshape(bsz, seq, nh))
            if last:
                yp = _out_final(o, None, 0, w_out, hp, gfin, tm=256)
            else:
                hp = _out_proj(o, None, 0, w_out, hp, tm=1024, tn=1024)

            qz = _proj_stack(hs, g, w_qz, (fox_scale, 1.0), tm=1024, tn=512, name="fox_qz_s")
            k, v, lf, kb, vb = _fox_kvf(hs, g, wk, wv, wf, bfp, nh=nh, tm=512)
            o = _fox_dec(page_table, qz, 0, lf, kb, vb, ck, cv, clf, j, nh=nh, nkv=nkv)
            outs["fk_s"].append(k.reshape(bd, nr, nkv, hd)[:, :ns])
            outs["fv_s"].append(v.reshape(bd, nr, nkv, hd)[:, :ns])
            outs["fl_s"].append(lf[:, :nh].reshape(bd, nr, nh)[:, :ns])
            if last:
                ys = _out_final(o, qz, 1, w_out, hs, gfin, tm=256)
            else:
                hs = _out_proj(o, qz, 1, w_out, hs, tm=512, tn=512)
        else:
            w = w_in_mla[j]
            o1, o2, o3 = ql, ql + kvl, ql + kvl + rope
            wqa, wkva = w[:, :o1].astype(BF16), w[:, o1:o2].astype(BF16)
            wkr = w[:, o2:o3]
            wkr2 = jnp.concatenate([_pad_last(wkr, LANES), _pad_last(_rot_cols(wkr), LANES)], axis=1).astype(BF16)
            wz = w[:, o3:].astype(BF16)[None]
            g = norm_mla[j].reshape(1, dm)
            qn, kvn = q_norm[j].reshape(1, ql), kv_norm[j].reshape(1, kvl)
            wqb = w_q_b[j]
            wn = wqb[:, :, :nope].reshape(ql, mh * nope).astype(BF16)
            wr = _pad_last(wqb[:, :, nope:], LANES).reshape(ql, mh * LANES).astype(BF16)
            wrr = _pad_last(_rot_cols(wqb[:, :, nope:]), LANES).reshape(ql, mh * LANES).astype(BF16)
            w_kv = jnp.stack([w_uk[j].reshape(kvl, mh * nope), w_uv[j].reshape(kvl, mh * vd)]).astype(BF16)
            w_ukt = jnp.transpose(w_uk[j], (1, 2, 0)).astype(BF16)
            w_uvh = jnp.transpose(w_uv[j], (1, 0, 2)).astype(BF16)
            w_out = w_out_mla[j].astype(BF16)

            z = _proj_stack(hp, g, wz, (1.0,), tm=1024, tn=1024, name="mla_z")
            qan, ckv, ckvb, kr, krp = _mla_a(hp, g, wqa, wkva, wkr2, qn, kvn, cos_p, sin_p, rope=rope, tm=512)
            qm, qx = _mla_q(qan, wn, wr, wrr, cos_p, sin_p, scale=mla_scale, tm=1024, hb=4)
            kvu = _proj_stack(ckvb, None, w_kv, (1.0, 1.0), tm=1024, tn=1024, name="mla_kv")
            o = _flash(qm.reshape(bsz, seq, mh * nope), qx.reshape(bsz, seq, mh * LANES),
                       kvu.reshape(2, bsz, seq, mh * nope), krp.reshape(bsz, seq, LANES),
                       kvu.reshape(2, bsz, seq, mh * vd), z.reshape(1, bsz, seq, mh * vd),
                       nkv=mh, group=1, tq=512, tk=512, qm_lead=None, km_lead=0, v_lead=1,
                       z_lead=0, ka_shared=True, name="mla_flash")
            o = o.reshape(mp, mh * vd)
            outs["mc_p"].append(ckv.reshape(bsz, seq, kvl))
            outs["mr_p"].append(kr.reshape(bsz, seq, rope))
            if last:
                yp = _out_final(o, None, 0, w_out, hp, gfin, tm=256)
            else:
                hp = _out_proj(o, None, 0, w_out, hp, tm=1024, tn=1024)

            z = _proj_stack(hs, g, wz, (1.0,), tm=1024, tn=512, name="mla_z_s")
            qan, ckv, ckvb, kr, krp = _mla_a(hs, g, wqa, wkva, wkr2, qn, kvn, cos_s, sin_s, rope=rope, tm=512)
            qm, qx = _mla_q(qan, wn, wr, wrr, cos_s, sin_s, scale=mla_scale, tm=1024, hb=4)
            qlat = _absorb(qm, w_ukt, tm=256)
            qx3 = qx.reshape(bd, nr, mh, LANES)[:, :ns].reshape(bd, ns * mh, LANES)
            olat = _mla_dec(page_table, qlat.reshape(kvl // LANES, bd, nr * mh, LANES), qx3,
                            ckvb, krp, cache_mla_ckv, ckrt, j, nh=mh, ns=ns)
            o = _unabsorb(olat.reshape(kvl // LANES, ms * mh, LANES), w_uvh, tm=128)
            outs["mc_s"].append(ckv.reshape(bd, nr, kvl)[:, :ns])
            outs["mr_s"].append(kr.reshape(bd, nr, rope)[:, :ns])
            if last:
                ys = _out_final(o, z, 0, w_out, hs, gfin, tm=256)
            else:
                hs = _out_proj(o, z, 0, w_out, hs, tm=512, tn=512)

    y_prompt = yp.reshape(bsz, seq, dm)
    y_sample = ys.reshape(bd, nr, dm)[:, :ns]
    st = lambda key: jnp.stack(outs[key])
    return (y_prompt, y_sample, st("fk_p"), st("fv_p"), st("fl_p"), st("fk_s"), st("fv_s"), st("fl_s"),
            st("mc_p"), st("mr_p"), st("mc_s"), st("mr_s"))
```

```python
import functools

import numpy as np
import jax
import jax.numpy as jnp
from jax import lax
from jax.experimental import pallas as pl
from jax.experimental.pallas import tpu as pltpu

F32 = jnp.float32
BF16 = jnp.bfloat16
EPS = 1e-6
ROPE_THETA = 10000.0
LOG2E = 1.4426950408889634
NEG = -1e30
LANES = 128
SUBLANES = 8
VMEM_LIMIT = 52 * 1024 * 1024
PAGES_PER_STEP = 8
MLA_PAGES_PER_STEP = 16
PAGE_SLOTS = 4


def _cparams(*sem):
    return pltpu.CompilerParams(dimension_semantics=sem, vmem_limit_bytes=VMEM_LIMIT)


def _rms(x, g):
    return x * lax.rsqrt(jnp.mean(x * x, axis=-1, keepdims=True) + EPS) * g


def _dot(a, b):
    return jnp.dot(a, b, preferred_element_type=F32)


def _dot_nt(a, b):
    return lax.dot_general(a, b, (((1,), (1,)), ((), ())), preferred_element_type=F32)


def _split3(x):
    hi = x.astype(BF16)
    r1 = x - hi.astype(F32)
    mid = r1.astype(BF16)
    lo = (r1 - mid.astype(F32)).astype(BF16)
    return hi, mid, lo


def _log_sigmoid(x):
    return jnp.minimum(x, 0.0) - jnp.log1p(jnp.exp(-jnp.abs(x)))


def _proj_kernel(*refs, norm, scales):
    if norm:
        x_ref, g_ref, w_ref, o_ref, h_ref = refs
    else:
        x_ref, w_ref, o_ref = refs
    gi = pl.program_id(1)
    if norm:
        @pl.when((gi == 0) & (pl.program_id(2) == 0))
        def _():
            h_ref[...] = _rms(x_ref[...], g_ref[...]).astype(BF16)
        h = h_ref[...]
    else:
        h = x_ref[...]
    acc = _dot(h, w_ref[...])
    if any(s != 1.0 for s in scales):
        sc = jnp.float32(scales[-1])
        for idx in range(len(scales) - 2, -1, -1):
            sc = jnp.where(gi == idx, jnp.float32(scales[idx]), sc)
        acc = acc * sc
    o_ref[...] = acc.astype(o_ref.dtype)


def _proj_stack(x, g, w, scales, *, tm, tn, name):
    m, kd = x.shape
    ng, _, n = w.shape
    norm = g is not None
    tm, tn = min(tm, m), min(tn, n)
    assert m % tm == 0 and n % tn == 0
    in_specs = [pl.BlockSpec((tm, kd), lambda i, k, j: (i, 0))]
    args = [x]
    scratch = []
    if norm:
        in_specs.append(pl.BlockSpec((1, kd), lambda i, k, j: (0, 0)))
        args.append(g)
        scratch.append(pltpu.VMEM((tm, kd), BF16))
    in_specs.append(pl.BlockSpec((None, kd, tn), lambda i, k, j: (k, 0, j)))
    args.append(w)
    return pl.pallas_call(
        functools.partial(_proj_kernel, norm=norm, scales=tuple(scales)),
        grid=(m // tm, ng, n // tn),
        in_specs=in_specs,
        out_specs=pl.BlockSpec((None, tm, tn), lambda i, k, j: (k, i, j)),
        out_shape=jax.ShapeDtypeStruct((ng, m, n), BF16),
        scratch_shapes=scratch,
        compiler_params=_cparams("parallel", "arbitrary", "arbitrary"),
        name=name,
    )(*args)


def _fox_kvf_kernel(x_ref, g_ref, wk_ref, wv_ref, wf_ref, bf_ref,
                    k_ref, v_ref, lf_ref, kb_ref, vb_ref, *, nh):
    h = _rms(x_ref[...], g_ref[...]).astype(BF16)
    k = _dot(h, wk_ref[...])
    v = _dot(h, wv_ref[...])
    tm = k.shape[0]
    nkv = k.shape[1] // LANES
    for g in range(nkv):
        k_ref[pl.ds(g, tm, stride=nkv), :] = k[:, g * LANES:(g + 1) * LANES]
        v_ref[pl.ds(g, tm, stride=nkv), :] = v[:, g * LANES:(g + 1) * LANES]
    kb_ref[...] = k.astype(BF16)
    vb_ref[...] = v.astype(BF16)
    lf = _log_sigmoid(_dot(h, wf_ref[...]) + bf_ref[...])
    lane = lax.broadcasted_iota(jnp.int32, lf.shape, 1)
    lf_ref[...] = jnp.where(lane < nh, lf, 0.0)


def _fox_kvf(x, g, wk, wv, wf, bf, *, nh, tm):
    m, kd = x.shape
    kvw = wk.shape[1]
    nkv = kvw // LANES
    tm = min(tm, m)
    assert m % tm == 0
    row = lambda i: (i, 0)
    fix = lambda i: (0, 0)
    return pl.pallas_call(
        functools.partial(_fox_kvf_kernel, nh=nh),
        grid=(m // tm,),
        in_specs=[pl.BlockSpec((tm, kd), row), pl.BlockSpec((1, kd), fix),
                  pl.BlockSpec((kd, kvw), fix), pl.BlockSpec((kd, kvw), fix),
                  pl.BlockSpec((kd, LANES), fix), pl.BlockSpec((1, LANES), fix)],
        out_specs=[pl.BlockSpec((tm * nkv, LANES), row), pl.BlockSpec((tm * nkv, LANES), row),
                   pl.BlockSpec((tm, LANES), row), pl.BlockSpec((tm, kvw), row),
                   pl.BlockSpec((tm, kvw), row)],
        out_shape=[jax.ShapeDtypeStruct((m * nkv, LANES), F32), jax.ShapeDtypeStruct((m * nkv, LANES), F32),
                   jax.ShapeDtypeStruct((m, LANES), F32), jax.ShapeDtypeStruct((m, kvw), BF16),
                   jax.ShapeDtypeStruct((m, kvw), BF16)],
        compiler_params=_cparams("parallel"),
        name="fox_kvf",
    )(x, g, wk, wv, wf, bf)


def _fox_aux_kernel(lf_ref, sq_ref, qc_ref, sk_ref, kc_ref, qa_ref, ka_ref, carry_ref):
    @pl.when(pl.program_id(1) == 0)
    def _():
        carry_ref[...] = jnp.zeros_like(carry_ref)
    lf = lf_ref[...]
    tc = lf.shape[0]
    r = lax.broadcasted_iota(jnp.int32, (tc, tc), 0)
    c = lax.broadcasted_iota(jnp.int32, (tc, tc), 1)
    tri = jnp.where(r >= c, 1.0, 0.0).astype(BF16)
    p = _split3(lf)
    cc = carry_ref[...] + (_dot(tri, p[0]) + _dot(tri, p[1]) + _dot(tri, p[2]))
    carry_ref[...] = cc[tc - 1:tc, :]
    p2 = _split3(cc * LOG2E)
    qa = qc_ref[...] + (_dot(p2[0], sq_ref[0]) + _dot(p2[1], sq_ref[1]) + _dot(p2[2], sq_ref[2]))
    ka = kc_ref[...] + (_dot(p2[0], sk_ref[0]) + _dot(p2[1], sk_ref[1]) + _dot(p2[2], sk_ref[2]))
    qa_ref[...] = qa.astype(BF16)
    ka_ref[...] = ka.astype(BF16)


def _fox_aux_tables(nh, nkv):
    grp = nh // nkv
    sq = np.zeros((3, LANES, nh * LANES), np.float32)
    qc = np.zeros((1, nh * LANES), np.float32)
    sk = np.zeros((3, LANES, nkv * LANES), np.float32)
    kc = np.zeros((1, nkv * LANES), np.float32)
    for h in range(nh):
        g, j = divmod(h, grp)
        for piece in range(3):
            sq[piece, h, h * LANES + piece] = 1.0
            qc[0, h * LANES + 3 + 3 * j + piece] = 1.0
            sk[piece, h, g * LANES + 3 + 3 * j + piece] = -1.0
    for g in range(nkv):
        kc[0, g * LANES:g * LANES + 3] = 1.0
    return (jnp.asarray(sq, BF16), jnp.asarray(qc, F32), jnp.asarray(sk, BF16), jnp.asarray(kc, F32))


def _fox_aux(lf, tables, *, nh, nkv, tc):
    b, s, _ = lf.shape
    sq, qc, sk, kc = tables
    tc = min(tc, s)
    assert s % tc == 0
    return pl.pallas_call(
        _fox_aux_kernel,
        grid=(b, s // tc),
        in_specs=[pl.BlockSpec((None, tc, LANES), lambda bi, i: (bi, i, 0)),
                  pl.BlockSpec(sq.shape, lambda bi, i: (0, 0, 0)),
                  pl.BlockSpec(qc.shape, lambda bi, i: (0, 0)),
                  pl.BlockSpec(sk.shape, lambda bi, i: (0, 0, 0)),
                  pl.BlockSpec(kc.shape, lambda bi, i: (0, 0))],
        out_specs=[pl.BlockSpec((None, tc, nh * LANES), lambda bi, i: (bi, i, 0)),
                   pl.BlockSpec((None, tc, nkv * LANES), lambda bi, i: (bi, i, 0))],
        out_shape=[jax.ShapeDtypeStruct((b, s, nh * LANES), BF16),
                   jax.ShapeDtypeStruct((b, s, nkv * LANES), BF16)],
        scratch_shapes=[pltpu.VMEM((1, LANES), F32)],
        compiler_params=_cparams("parallel", "arbitrary"),
        name="fox_aux",
    )(lf, sq, qc, sk, kc)


def _silu(z):
    return z * (1.0 / (1.0 + jnp.exp(-z)))


def _flash_kernel(qm_ref, qa_ref, km_ref, ka_ref, v_ref, z_ref, o_ref,
                  q_scr, s0_scr, s1_scr, m_scr, acc_scr, *, group, tq, tk, hd):
    i = pl.program_id(2)
    rows = group * tq
    per = tq // tk
    s_bufs = (s0_scr, s1_scr)
    for j in range(group):
        q_scr[j * tq:(j + 1) * tq, 0:hd] = qm_ref[:, j * hd:(j + 1) * hd]
        q_scr[j * tq:(j + 1) * tq, hd:2 * hd] = qa_ref[:, j * hd:(j + 1) * hd]
    m_scr[...] = jnp.full_like(m_scr, NEG)
    acc_scr[...] = jnp.zeros_like(acc_scr)
    ones = jnp.ones((tk, hd), BF16)

    def scores(step, slot, mask_off):
        ks = pl.multiple_of(step * tk, tk)
        k = jnp.concatenate([km_ref[pl.ds(ks, tk), :], ka_ref[pl.ds(ks, tk), :]], axis=1)
        s = _dot_nt(q_scr[...], k)
        if mask_off is not None:
            t = lax.broadcasted_iota(jnp.int32, (rows, tk), 0) & (tq - 1)
            col = lax.broadcasted_iota(jnp.int32, (rows, tk), 1) + mask_off
            s = jnp.where(col <= t, s, NEG)
        s_bufs[slot][...] = s

    def consume(step, slot):
        ks = pl.multiple_of(step * tk, tk)
        s_ref = s_bufs[slot]
        m_prev = m_scr[...]
        m_new = jnp.maximum(m_prev, jnp.max(s_ref[...], axis=1, keepdims=True))
        alpha = jnp.exp2(m_prev - m_new)
        p = jnp.exp2(s_ref[...] - jnp.tile(m_new, (1, tk // hd)))
        v1 = jnp.concatenate([v_ref[pl.ds(ks, tk), :], ones], axis=1)
        acc_scr[...] = jnp.tile(alpha, (1, 2)) * acc_scr[...] + _dot(p.astype(BF16), v1)
        m_scr[...] = m_new

    def tail(first, n_plain):
        seq = [(first + n, None) for n in range(n_plain)]
        seq += [(first + n_plain + d, d * tk) for d in range(per)]
        for n, (step, _) in enumerate(seq):
            if n + 1 < len(seq):
                scores(seq[n + 1][0], (n + 1) % 2, seq[n + 1][1])
            consume(step, n % 2)

    n_plain = i * per

    @pl.when(i == 0)
    def _():
        scores(0, 0, 0)
        tail(0, 0)

    @pl.when(i > 0)
    def _():
        scores(0, 0, None)
        n_pairs = (n_plain - 1) // 2

        def body(jj, carry):
            scores(2 * jj + 1, 1, None)
            consume(2 * jj, 0)
            scores(2 * jj + 2, 0, None)
            consume(2 * jj + 1, 1)
            return carry

        lax.fori_loop(0, n_pairs, body, 0)
        if per % 2 == 0:
            tail(2 * n_pairs, 2)
        else:
            @pl.when(n_plain % 2 == 1)
            def _():
                tail(2 * n_pairs, 1)

            @pl.when(n_plain % 2 == 0)
            def _():
                tail(2 * n_pairs, 2)

    acc = acc_scr[...]
    o = acc[:, :hd] / acc[:, hd:]
    for j in range(group):
        gate = _silu(z_ref[:, j * hd:(j + 1) * hd].astype(F32))
        o_ref[:, j * hd:(j + 1) * hd] = (o[j * tq:(j + 1) * tq, :] * gate).astype(o_ref.dtype)


def _flash(qm, qa, km, ka, v, z, *, nkv, group, tq, tk, qm_lead, km_lead, v_lead, z_lead, ka_shared, name):
    hd = LANES
    b, s = qa.shape[0], qa.shape[1]
    tq, tk = min(tq, s), min(tk, s)
    assert s % tq == 0 and tq % tk == 0

    def spec(lead, blk_rows, width, imap):
        if lead is None:
            return pl.BlockSpec((None, blk_rows, width), imap)
        return pl.BlockSpec((None, None, blk_rows, width), lambda bi, g, i: (lead,) + imap(bi, g, i))

    q_map = lambda bi, g, i: (bi, i, g)
    k_map = lambda bi, g, i: (bi, 0, g)
    ka_map = (lambda bi, g, i: (bi, 0, 0)) if ka_shared else k_map
    return pl.pallas_call(
        functools.partial(_flash_kernel, group=group, tq=tq, tk=tk, hd=hd),
        grid=(b, nkv, s // tq),
        in_specs=[spec(qm_lead, tq, group * hd, q_map), spec(None, tq, group * hd, q_map),
                  spec(km_lead, s, hd, k_map), spec(None, s, hd, ka_map), spec(v_lead, s, hd, k_map),
                  spec(z_lead, tq, group * hd, q_map)],
        out_specs=pl.BlockSpec((None, tq, group * hd), q_map),
        out_shape=jax.ShapeDtypeStruct((b, s, nkv * group * hd), BF16),
        scratch_shapes=[pltpu.VMEM((group * tq, 2 * hd), BF16), pltpu.VMEM((group * tq, tk), F32),
                        pltpu.VMEM((group * tq, tk), F32), pltpu.VMEM((group * tq, hd), F32),
                        pltpu.VMEM((group * tq, 2 * hd), F32)],
        compiler_params=_cparams("parallel", "parallel", "arbitrary"),
        name=name,
    )(qm, qa, km, ka, v, z)


def _gated(o_ref, z_ref):
    if z_ref is None:
        return o_ref[...]
    return (o_ref[...].astype(F32) * _silu(z_ref[...].astype(F32))).astype(BF16)


def _out_proj_kernel(*refs, gated):
    if gated:
        o_ref, z_ref, w_ref, r_ref, y_ref, u_scr = refs

        @pl.when(pl.program_id(1) == 0)
        def _():
            u_scr[...] = _gated(o_ref, z_ref)
        u = u_scr[...]
    else:
        o_ref, w_ref, r_ref, y_ref = refs
        u = o_ref[...]
    y_ref[...] = r_ref[...] + _dot(u, w_ref[...])


def _out_proj(o, z, z_lead, w, resid, *, tm, tn):
    m, kd = o.shape
    n = w.shape[1]
    tm, tn = min(tm, m), min(tn, n)
    assert m % tm == 0 and n % tn == 0
    gated = z is not None
    in_specs = [pl.BlockSpec((tm, kd), lambda i, j: (i, 0))]
    args = [o]
    if gated:
        in_specs.append(pl.BlockSpec((None, tm, kd), lambda i, j: (z_lead, i, 0)))
        args.append(z)
    in_specs += [pl.BlockSpec((kd, tn), lambda i, j: (0, j)), pl.BlockSpec((tm, tn), lambda i, j: (i, j))]
    return pl.pallas_call(
        functools.partial(_out_proj_kernel, gated=gated),
        grid=(m // tm, n // tn),
        in_specs=in_specs,
        out_specs=pl.BlockSpec((tm, tn), lambda i, j: (i, j)),
        out_shape=jax.ShapeDtypeStruct((m, n), F32),
        scratch_shapes=[pltpu.VMEM((tm, kd), BF16)] if gated else [],
        compiler_params=_cparams("parallel", "arbitrary"),
        name="out_proj",
    )(*args, w, resid)


def _out_final_kernel(*refs, gated):
    if gated:
        o_ref, z_ref, w_ref, r_ref, g_ref, y_ref = refs
    else:
        o_ref, w_ref, r_ref, g_ref, y_ref = refs
        z_ref = None
    hsum = r_ref[...] + _dot(_gated(o_ref, z_ref), w_ref[...])
    y_ref[...] = _rms(hsum, g_ref[...])


def _out_final(o, z, z_lead, w, resid, gfin, *, tm):
    m, kd = o.shape
    n = w.shape[1]
    tm = min(tm, m)
    assert m % tm == 0
    gated = z is not None
    in_specs = [pl.BlockSpec((tm, kd), lambda i: (i, 0))]
    args = [o]
    if gated:
        in_specs.append(pl.BlockSpec((None, tm, kd), lambda i: (z_lead, i, 0)))
        args.append(z)
    in_specs += [pl.BlockSpec((kd, n), lambda i: (0, 0)), pl.BlockSpec((tm, n), lambda i: (i, 0)),
                 pl.BlockSpec((1, n), lambda i: (0, 0))]
    return pl.pallas_call(
        functools.partial(_out_final_kernel, gated=gated),
        grid=(m // tm,),
        in_specs=in_specs,
        out_specs=pl.BlockSpec((tm, n), lambda i: (i, 0)),
        out_shape=jax.ShapeDtypeStruct((m, n), F32),
        compiler_params=_cparams("parallel"),
        name="out_final",
    )(*args, w, resid, gfin)


def _mla_a_kernel(x_ref, g_ref, wqa_ref, wkva_ref, wkr_ref, qn_ref, kvn_ref, cos_ref, sin_ref,
                  qa_ref, ckv_ref, ckvb_ref, kr_ref, krp_ref, *, rope):
    h = _rms(x_ref[...], g_ref[...]).astype(BF16)
    qa_ref[...] = _rms(_dot(h, wqa_ref[...]), qn_ref[...]).astype(BF16)
    ckv = _rms(_dot(h, wkva_ref[...]), kvn_ref[...])
    ckv_ref[...] = ckv
    ckvb_ref[...] = ckv.astype(BF16)
    kr2 = _dot(h, wkr_ref[...])
    krp = kr2[:, :LANES] * cos_ref[...] + kr2[:, LANES:] * sin_ref[...]
    kr_ref[...] = krp[:, :rope]
    krp_ref[...] = krp.astype(BF16)


def _mla_a(x, g, wqa, wkva, wkr2, qn, kvn, cos, sin, *, rope, tm):
    m, kd = x.shape
    ql, kvl = wqa.shape[1], wkva.shape[1]
    tm = min(tm, m)
    assert m % tm == 0
    nt = cos.shape[0] // tm
    row = lambda i: (i, 0)
    fix = lambda i: (0, 0)
    tab = lambda i: (i % nt, 0)
    return pl.pallas_call(
        functools.partial(_mla_a_kernel, rope=rope),
        grid=(m // tm,),
        in_specs=[pl.BlockSpec((tm, kd), row), pl.BlockSpec((1, kd), fix),
                  pl.BlockSpec((kd, ql), fix), pl.BlockSpec((kd, kvl), fix),
                  pl.BlockSpec((kd, 2 * LANES), fix), pl.BlockSpec((1, ql), fix),
                  pl.BlockSpec((1, kvl), fix), pl.BlockSpec((tm, LANES), tab),
                  pl.BlockSpec((tm, LANES), tab)],
        out_specs=[pl.BlockSpec((tm, ql), row), pl.BlockSpec((tm, kvl), row),
                   pl.BlockSpec((tm, kvl), row), pl.BlockSpec((tm, rope), row),
                   pl.BlockSpec((tm, LANES), row)],
        out_shape=[jax.ShapeDtypeStruct((m, ql), BF16), jax.ShapeDtypeStruct((m, kvl), F32),
                   jax.ShapeDtypeStruct((m, kvl), BF16), jax.ShapeDtypeStruct((m, rope), F32),
                   jax.ShapeDtypeStruct((m, LANES), BF16)],
        compiler_params=_cparams("parallel"),
        name="mla_a",
    )(x, g, wqa, wkva, wkr2, qn, kvn, cos, sin)


def _mla_q_kernel(x_ref, wn_ref, wr_ref, wrr_ref, cos_ref, sin_ref, qm_ref, qa_ref, *, hb, scale):
    x = x_ref[...]
    qm_ref[...] = (_dot(x, wn_ref[...]) * scale).astype(BF16)
    qr = _dot(x, wr_ref[...])
    qrr = _dot(x, wrr_ref[...])
    cos = cos_ref[...]
    sin = sin_ref[...]
    for jj in range(hb):
        sl = slice(jj * LANES, (jj + 1) * LANES)
        qa_ref[:, sl] = ((qr[:, sl] * cos + qrr[:, sl] * sin) * scale).astype(BF16)


def _mla_q(x, wn, wr, wrr, cos, sin, *, scale, tm, hb):
    m, kd = x.shape
    n = wn.shape[1]
    tm = min(tm, m)
    assert m % tm == 0
    tn = hb * LANES
    nt = cos.shape[0] // tm
    wmap = lambda i, j: (0, j)
    omap = lambda i, j: (i, j)
    tab = lambda i, j: (i % nt, 0)
    return pl.pallas_call(
        functools.partial(_mla_q_kernel, hb=hb, scale=scale),
        grid=(m // tm, n // tn),
        in_specs=[pl.BlockSpec((tm, kd), lambda i, j: (i, 0)),
                  pl.BlockSpec((kd, tn), wmap), pl.BlockSpec((kd, tn), wmap), pl.BlockSpec((kd, tn), wmap),
                  pl.BlockSpec((tm, LANES), tab), pl.BlockSpec((tm, LANES), tab)],
        out_specs=[pl.BlockSpec((tm, tn), omap), pl.BlockSpec((tm, tn), omap)],
        out_shape=[jax.ShapeDtypeStruct((m, n), BF16), jax.ShapeDtypeStruct((m, n), BF16)],
        compiler_params=_cparams("parallel", "arbitrary"),
        name="mla_q",
    )(x, wn, wr, wrr, cos, sin)


def _absorb_kernel(x_ref, w_ref, o_ref, *, nh):
    tm = x_ref.shape[0]
    kd = x_ref.shape[1] // nh
    for h in range(nh):
        y = _dot(x_ref[:, h * kd:(h + 1) * kd], w_ref[h])
        for c in range(o_ref.shape[0]):
            o_ref[c, pl.ds(h, tm, stride=nh), :] = y[:, c * LANES:(c + 1) * LANES]


def _absorb(x, w, *, tm):
    m = x.shape[0]
    nh, kd, n = w.shape
    tm = min(tm, m)
    assert m % tm == 0
    return pl.pallas_call(
        functools.partial(_absorb_kernel, nh=nh),
        grid=(m // tm,),
        in_specs=[pl.BlockSpec((tm, nh * kd), lambda i: (i, 0)), pl.BlockSpec((nh, kd, n), lambda i: (0, 0, 0))],
        out_specs=pl.BlockSpec((n // LANES, tm * nh, LANES), lambda i: (0, i, 0)),
        out_shape=jax.ShapeDtypeStruct((n // LANES, m * nh, LANES), F32),
        compiler_params=_cparams("parallel"),
        name="mla_absorb",
    )(x, w)


def _unabsorb_kernel(x_ref, w_ref, o_ref, *, nh):
    tm = o_ref.shape[0]
    n = o_ref.shape[1] // nh
    for h in range(nh):
        xh = jnp.concatenate([x_ref[c, pl.ds(h, tm, stride=nh), :] for c in range(x_ref.shape[0])], axis=1)
        o_ref[:, h * n:(h + 1) * n] = _dot(xh.astype(BF16), w_ref[h]).astype(o_ref.dtype)


def _unabsorb(x, w, *, tm):
    nh, kd, n = w.shape
    m = x.shape[1] // nh
    tm = min(tm, m)
    assert m % tm == 0
    return pl.pallas_call(
        functools.partial(_unabsorb_kernel, nh=nh),
        grid=(m // tm,),
        in_specs=[pl.BlockSpec((kd // LANES, tm * nh, LANES), lambda i: (0, i, 0)),
                  pl.BlockSpec((nh, kd, n), lambda i: (0, 0, 0))],
        out_specs=pl.BlockSpec((tm, nh * n), lambda i: (i, 0)),
        out_shape=jax.ShapeDtypeStruct((m, nh * n), BF16),
        compiler_params=_cparams("parallel"),
        name="mla_uv",
    )(x, w)


def _softmax_step(s, v, m_scr, l_scr, acc_scr, idx):
    m_prev = m_scr[idx]
    m_new = jnp.maximum(m_prev, jnp.max(s, axis=1, keepdims=True))
    alpha = jnp.exp2(m_prev - m_new)
    p = jnp.exp2(s - m_new)
    l_scr[idx] = alpha * l_scr[idx] + jnp.sum(p, axis=1, keepdims=True)
    acc_scr[idx] = alpha * acc_scr[idx] + _dot(p.astype(BF16), v)
    m_scr[idx] = m_new


def _pad_rows(x, rows):
    return jnp.concatenate([x, jnp.zeros((rows - x.shape[0], x.shape[1]), x.dtype)], axis=0)


def _page_copies(tab_ref, seq, first_page, layer, srcs, bufs, sems, slot, npg):
    copies = []
    for p in range(npg):
        pg = tab_ref[seq, first_page + p]
        for a, (src, buf) in enumerate(zip(srcs, bufs)):
            copies.append(pltpu.make_async_copy(src.at[layer, pg], buf.at[slot, p], sems.at[a, slot]))
    return copies


def _paged_pipeline(tab_ref, srcs, bufs, sems, *, layer, npg, nc, reverse, on_chunk):
    b, c2 = pl.program_id(0), pl.program_id(1)
    total = pl.num_programs(0) * nc
    n0 = (b * (nc // PAGE_SLOTS) + c2) * PAGE_SLOTS

    def copies(n, slot):
        seq, cc = n // nc, n % nc
        first = ((nc - 1 - cc) if reverse else cc) * npg
        return _page_copies(tab_ref, seq, first, layer, srcs, bufs, sems, slot, npg)

    @pl.when(n0 == 0)
    def _():
        for slot in range(PAGE_SLOTS):
            for cp in copies(slot, slot):
                cp.start()

    for slot in range(PAGE_SLOTS):
        n = n0 + slot
        for cp in copies(n, slot):
            cp.wait()
        on_chunk(slot)

        @pl.when(n + PAGE_SLOTS < total)
        def _():
            for cp in copies(n + PAGE_SLOTS, slot):
                cp.start()


def _fox_dec_kernel(tab_ref, q_ref, lfn_ref, kn_ref, vn_ref, ck_ref, cv_ref, clf_ref, o_ref,
                    kbuf, vbuf, lbuf, sems, m_scr, l_scr, acc_scr, car_scr, cn_scr,
                    *, layer, npg, nc, nkv, grp, page):
    c2 = pl.program_id(1)
    hd = LANES
    nr = SUBLANES
    grows = grp * nr

    def q_pair(pr):
        blocks = []
        for gg in range(2):
            g = 2 * pr + gg
            qg = jnp.concatenate([q_ref[:, (g * grp + j) * hd:(g * grp + j + 1) * hd] for j in range(grp)], axis=0)
            zero = jnp.zeros_like(qg)
            blocks.append(jnp.concatenate([qg, zero] if gg == 0 else [zero, qg], axis=1))
        return jnp.concatenate(blocks, axis=0)

    def biased(s, pr, col_bias, row_bias, sign, mask):
        parts = []
        for gg in range(2):
            for j in range(grp):
                h = (2 * pr + gg) * grp + j
                r0 = gg * grows + j * nr
                sj = s[r0:r0 + nr, :] + col_bias[:, h:h + 1] + sign * row_bias[h:h + 1, :]
                parts.append(sj if mask is None else jnp.where(mask, sj, NEG))
        return jnp.concatenate(parts, axis=0)

    @pl.when(c2 == 0)
    def _():
        m_scr[...] = jnp.full_like(m_scr, NEG)
        l_scr[...] = jnp.zeros_like(l_scr)
        acc_scr[...] = jnp.zeros_like(acc_scr)
        car_scr[...] = jnp.zeros_like(car_scr)
        lf = lfn_ref[...]
        rid = lax.broadcasted_iota(jnp.int32, lf.shape, 0)
        cn = jnp.zeros_like(lf)
        for t in range(nr):
            cn = cn + jnp.where(rid >= t, lf[t:t + 1, :], 0.0)
        cn = cn * LOG2E
        cn_scr[...] = cn
        cnt = _pad_rows(cn, LANES).T
        mask = (lax.broadcasted_iota(jnp.int32, (nr, LANES), 1)
                <= lax.broadcasted_iota(jnp.int32, (nr, LANES), 0))
        for pr in range(nkv // 2):
            kk = _pad_rows(kn_ref[:, 2 * pr * hd:(2 * pr + 2) * hd], LANES)
            vv = _pad_rows(vn_ref[:, 2 * pr * hd:(2 * pr + 2) * hd], LANES)
            s = biased(_dot_nt(q_pair(pr), kk), pr, cn, cnt, -1.0, mask)
            _softmax_step(s, vv, m_scr, l_scr, acc_scr, pr)

    def on_chunk(slot):
        lfs = [lbuf[slot, p] for p in range(npg)]
        pieces = [_split3(x) for x in lfs]
        stacked = jnp.concatenate([pieces[p][k] for k in range(3) for p in range(npg)], axis=0)
        r = lax.broadcasted_iota(jnp.int32, (page, page), 0)
        cc = lax.broadcasted_iota(jnp.int32, (page, page), 1)
        upper = jnp.where(r > cc, 1.0, 0.0).astype(BF16)
        res = _dot(stacked, upper)
        nh = lfs[0].shape[0]
        off = car_scr[...]
        dks = [None] * npg
        for p in range(npg - 1, -1, -1):
            loc = (res[p * nh:(p + 1) * nh] + res[(npg + p) * nh:(npg + p + 1) * nh]
                   + res[(2 * npg + p) * nh:(2 * npg + p + 1) * nh])
            dks[p] = (loc + off) * LOG2E
            off = off + (loc[:, 0:1] + lfs[p][:, 0:1])
        car_scr[...] = off
        dk = jnp.concatenate(dks, axis=1)
        cn = cn_scr[...]

        def cached_pair(buf, pr):
            return jnp.concatenate(
                [jnp.concatenate([buf[slot, p, pl.ds(2 * pr + gg, page, stride=nkv), :] for gg in range(2)], axis=1)
                 for p in range(npg)], axis=0).astype(BF16)

        for pr in range(nkv // 2):
            s = biased(_dot_nt(q_pair(pr), cached_pair(kbuf, pr)), pr, cn, dk, 1.0, None)
            _softmax_step(s, cached_pair(vbuf, pr), m_scr, l_scr, acc_scr, pr)

    _paged_pipeline(tab_ref, (ck_ref, cv_ref, clf_ref), (kbuf, vbuf, lbuf), sems,
                    layer=layer, npg=npg, nc=nc, reverse=True, on_chunk=on_chunk)

    @pl.when(c2 == pl.num_programs(1) - 1)
    def _():
        for pr in range(nkv // 2):
            o = acc_scr[pr] / l_scr[pr]
            for gg in range(2):
                for j in range(grp):
                    h = (2 * pr + gg) * grp + j
                    r0 = gg * grows + j * nr
                    o_ref[:, h * hd:(h + 1) * hd] = o[r0:r0 + nr, gg * hd:(gg + 1) * hd].astype(o_ref.dtype)


def _fox_dec(page_table, q, q_lead, lfn, kn, vn, ck, cv, clf, layer, *, nh, nkv):
    bd, n_pages = page_table.shape
    npg = min(PAGES_PER_STEP, n_pages)
    nc = n_pages // npg
    assert nc % PAGE_SLOTS == 0 and nkv % 2 == 0
    page = clf.shape[-1]
    hd = LANES
    nr = SUBLANES
    grp = nh // nkv
    row = lambda b, c, tab: (b, 0)
    hbm = pl.BlockSpec(memory_space=pl.ANY)
    in_specs = [pl.BlockSpec((None, nr, nh * hd), lambda b, c, tab: (q_lead, b, 0)),
                pl.BlockSpec((nr, LANES), row), pl.BlockSpec((nr, nkv * hd), row),
                pl.BlockSpec((nr, nkv * hd), row), hbm, hbm, hbm]
    rows = 2 * grp * nr
    return pl.pallas_call(
        functools.partial(_fox_dec_kernel, layer=layer, npg=npg, nc=nc, nkv=nkv, grp=grp, page=page),
        grid_spec=pltpu.PrefetchScalarGridSpec(
            num_scalar_prefetch=1, grid=(bd, nc // PAGE_SLOTS), in_specs=in_specs,
            out_specs=pl.BlockSpec((nr, nh * hd), row),
            scratch_shapes=[pltpu.VMEM((PAGE_SLOTS, npg, page * nkv, hd), F32), pltpu.VMEM((PAGE_SLOTS, npg, page * nkv, hd), F32),
                            pltpu.VMEM((PAGE_SLOTS, npg, nh, page), F32), pltpu.SemaphoreType.DMA((3, PAGE_SLOTS)),
                            pltpu.VMEM((nkv // 2, rows, 1), F32), pltpu.VMEM((nkv // 2, rows, 1), F32),
                            pltpu.VMEM((nkv // 2, rows, 2 * hd), F32), pltpu.VMEM((nh, 1), F32),
                            pltpu.VMEM((nr, LANES), F32)]),
        out_shape=jax.ShapeDtypeStruct((bd * nr, nh * hd), BF16),
        compiler_params=_cparams("arbitrary", "arbitrary"),
        name="fox_dec",
    )(page_table, q, lfn, kn, vn, ck, cv, clf)


def _mla_dec_kernel(tab_ref, ql_ref, qa_ref, cn_ref, krn_ref, cc_ref, cr_ref, o_ref,
                    cbuf, rbuf, sems, m_scr, l_scr, acc_scr, *, layer, npg, nc, nh, ns, rope):
    c2 = pl.program_id(1)
    rows = ns * nh
    nslab = ql_ref.shape[0]
    ql = jnp.concatenate([ql_ref[c] for c in range(nslab)], axis=1).astype(BF16)
    qa = qa_ref[...]

    @pl.when(c2 == 0)
    def _():
        m_scr[...] = jnp.full_like(m_scr, NEG)
        l_scr[...] = jnp.zeros_like(l_scr)
        acc_scr[...] = jnp.zeros_like(acc_scr)
        cnew = _pad_rows(cn_ref[...], LANES)
        s = _dot_nt(ql, cnew) + _dot_nt(qa, _pad_rows(krn_ref[...], LANES))
        t_id = lax.broadcasted_iota(jnp.int32, (ns, nh, LANES), 0).reshape(rows, LANES)
        s_id = lax.broadcasted_iota(jnp.int32, (rows, LANES), 1)
        _softmax_step(jnp.where(s_id <= t_id, s, NEG), cnew, m_scr, l_scr, acc_scr, 0)

    def on_chunk(slot):
        ckv = jnp.concatenate([cbuf[slot, p] for p in range(npg)], axis=0).astype(BF16)
        krt = jnp.concatenate([rbuf[slot, p] for p in range(npg)], axis=1).astype(BF16)
        s = _dot_nt(ql, ckv) + _dot(qa[:, :rope], krt)
        _softmax_step(s, ckv, m_scr, l_scr, acc_scr, 0)

    _paged_pipeline(tab_ref, (cc_ref, cr_ref), (cbuf, rbuf), sems,
                    layer=layer, npg=npg, nc=nc, reverse=False, on_chunk=on_chunk)

    @pl.when(c2 == pl.num_programs(1) - 1)
    def _():
        o = acc_scr[0] / l_scr[0]
        for c in range(nslab):
            o_ref[c, 0:rows, :] = o[:, c * LANES:(c + 1) * LANES]
            o_ref[c, rows:, :] = jnp.zeros((o_ref.shape[1] - rows, LANES), o_ref.dtype)


def _mla_dec(page_table, qlat, qaux, cnew, krnew, cckv, ckrt, layer, *, nh, ns):
    bd, n_pages = page_table.shape
    npg = min(MLA_PAGES_PER_STEP, n_pages)
    nc = n_pages // npg
    assert nc % PAGE_SLOTS == 0
    page, kvl = cckv.shape[-2], cckv.shape[-1]
    rope = ckrt.shape[-2]
    nr = SUBLANES
    rows = ns * nh
    row = lambda b, c, tab: (b, 0)
    seq = lambda b, c, tab: (b, 0, 0)
    hbm = pl.BlockSpec(memory_space=pl.ANY)
    nslab = kvl // LANES
    slab = lambda b, c, tab: (0, b, 0, 0)
    in_specs = [pl.BlockSpec((nslab, None, rows, LANES), slab), pl.BlockSpec((None, rows, LANES), seq),
                pl.BlockSpec((nr, kvl), row), pl.BlockSpec((nr, LANES), row), hbm, hbm]
    return pl.pallas_call(
        functools.partial(_mla_dec_kernel, layer=layer, npg=npg, nc=nc, nh=nh, ns=ns, rope=rope),
        grid_spec=pltpu.PrefetchScalarGridSpec(
            num_scalar_prefetch=1, grid=(bd, nc // PAGE_SLOTS), in_specs=in_specs,
            out_specs=pl.BlockSpec((nslab, None, nr * nh, LANES), slab),
            scratch_shapes=[pltpu.VMEM((PAGE_SLOTS, npg, page, kvl), F32), pltpu.VMEM((PAGE_SLOTS, npg, rope, page), F32),
                            pltpu.SemaphoreType.DMA((2, PAGE_SLOTS)),
                            pltpu.VMEM((1, rows, 1), F32), pltpu.VMEM((1, rows, 1), F32),
                            pltpu.VMEM((1, rows, kvl), F32)]),
        out_shape=jax.ShapeDtypeStruct((nslab, bd, nr * nh, LANES), F32),
        compiler_params=_cparams("arbitrary", "arbitrary"),
        name="mla_dec",
    )(page_table, qlat, qaux, cnew, krnew, cckv, ckrt)


def _rope_tables(pos, rope):
    half = rope // 2
    inv = 1.0 / (ROPE_THETA ** (jnp.arange(half, dtype=F32) * (2.0 / rope)))
    ang = pos.astype(F32)[:, None] * inv[None, :]
    zeros = jnp.zeros((pos.shape[0], LANES - rope), F32)
    cos = jnp.concatenate([jnp.cos(ang), jnp.cos(ang), zeros], axis=1)
    sin = jnp.concatenate([jnp.sin(ang), jnp.sin(ang), zeros], axis=1)
    return cos, sin


def _rot_cols(w):
    half = w.shape[-1] // 2
    return jnp.concatenate([-w[..., half:], w[..., :half]], axis=-1)


def _pad_last(w, width):
    pad = [(0, 0)] * (w.ndim - 1) + [(0, width - w.shape[-1])]
    return jnp.pad(w, pad)


def kernel(x_prompt, x_sample, cache_fox_k, cache_fox_v, cache_fox_logf, cache_mla_ckv, cache_mla_krope, page_table, norm_fox, w_in_fox, b_f, w_out_fox, norm_mla, w_in_mla, q_norm, w_q_b, kv_norm, w_uk, w_uv, w_out_mla, final_norm):
    bsz, seq, dm = x_prompt.shape
    bd, ns, _ = x_sample.shape
    nf, n_pool, page, nkv, hd = cache_fox_k.shape
    nm = cache_mla_ckv.shape[0]
    nh = b_f.shape[1]
    grp = nh // nkv
    fw, kvw = nh * hd, nkv * hd
    n_pages = page_table.shape[1]
    past = n_pages * page
    _, ql, mh, qk = w_q_b.shape
    kvl, nope = w_uk.shape[1], w_uk.shape[3]
    rope = qk - nope
    vd = w_uv.shape[3]
    assert hd == LANES and nope == LANES and vd == LANES and rope <= LANES and ns <= SUBLANES
    nr = SUBLANES
    mp, ms = bsz * seq, bd * nr

    hp = x_prompt.reshape(mp, dm)
    hs = jnp.pad(x_sample, ((0, 0), (0, nr - ns), (0, 0))).reshape(ms, dm)

    ck = cache_fox_k.reshape(nf, n_pool, page * nkv, hd)
    cv = cache_fox_v.reshape(nf, n_pool, page * nkv, hd)
    clf = jnp.swapaxes(cache_fox_logf, 2, 3)
    ckrt = jnp.swapaxes(cache_mla_krope, 2, 3)

    cos_p, sin_p = _rope_tables(jnp.arange(seq), rope)
    cos_s, sin_s = _rope_tables(past + jnp.arange(nr), rope)
    cos_s, sin_s = jnp.tile(cos_s, (bd, 1)), jnp.tile(sin_s, (bd, 1))
    aux_tables = _fox_aux_tables(nh, nkv)
    fox_scale = hd ** -0.5 * LOG2E
    mla_scale = qk ** -0.5 * LOG2E
    gfin = final_norm.reshape(1, dm)

    outs = {k: [] for k in ("fk_p", "fv_p", "fl_p", "fk_s", "fv_s", "fl_s", "mc_p", "mr_p", "mc_s", "mr_s")}
    depth = nf + nm
    yp = ys = None
    for i in range(depth):
        j = i // 2
        last = i == depth - 1
        if i % 2 == 0:
            w = w_in_fox[j]
            o1, o2, o3, o4 = fw, fw + kvw, fw + 2 * kvw, fw + 2 * kvw + nh
            w_qz = jnp.stack([w[:, :o1], w[:, o4:]]).astype(BF16)
            wk, wv = w[:, o1:o2].astype(BF16), w[:, o2:o3].astype(BF16)
            wf = _pad_last(w[:, o3:o4], LANES).astype(BF16)
            bfp = _pad_last(b_f[j].reshape(1, nh), LANES)
            g = norm_fox[j].reshape(1, dm)
            w_out = w_out_fox[j].astype(BF16)

            qz = _proj_stack(hp, g, w_qz, (fox_scale, 1.0), tm=1024, tn=1024, name="fox_qz")
            k, v, lf, kb, vb = _fox_kvf(hp, g, wk, wv, wf, bfp, nh=nh, tm=512)
            qa, ka = _fox_aux(lf.reshape(bsz, seq, LANES), aux_tables, nh=nh, nkv=nkv, tc=512)
            qz4 = qz.reshape(2, bsz, seq, fw)
            o = _flash(qz4, qa, kb.reshape(bsz, seq, kvw), ka, vb.reshape(bsz, seq, kvw), qz4,
                       nkv=nkv, group=grp, tq=512, tk=512, qm_lead=0, km_lead=None, v_lead=None,
                       z_lead=1, ka_shared=False, name="fox_flash")
            o = o.reshape(mp, fw)
            outs["fk_p"].append(k.reshape(bsz, seq, nkv, hd))
            outs["fv_p"].append(v.reshape(bsz, seq, nkv, hd))
            outs["fl_p"].append(lf[:, :nh].reshape(bsz, seq, nh))
            if last:
                yp = _out_final(o, None, 0, w_out, hp, gfin, tm=256)
            else:
                hp = _out_proj(o, None, 0, w_out, hp, tm=1024, tn=1024)

            qz = _proj_stack(hs, g, w_qz, (fox_scale, 1.0), tm=1024, tn=512, name="fox_qz_s")
            k, v, lf, kb, vb = _fox_kvf(hs, g, wk, wv, wf, bfp, nh=nh, tm=512)
            o = _fox_dec(page_table, qz, 0, lf, kb, vb, ck, cv, clf, j, nh=nh, nkv=nkv)
            outs["fk_s"].append(k.reshape(bd, nr, nkv, hd)[:, :ns])
            outs["fv_s"].append(v.reshape(bd, nr, nkv, hd)[:, :ns])
            outs["fl_s"].append(lf[:, :nh].reshape(bd, nr, nh)[:, :ns])
            if last:
                ys = _out_final(o, qz, 1, w_out, hs, gfin, tm=256)
            else:
                hs = _out_proj(o, qz, 1, w_out, hs, tm=512, tn=512)
        else:
            w = w_in_mla[j]
            o1, o2, o3 = ql, ql + kvl, ql + kvl + rope
            wqa, wkva = w[:, :o1].astype(BF16), w[:, o1:o2].astype(BF16)
            wkr = w[:, o2:o3]
            wkr2 = jnp.concatenate([_pad_last(wkr, LANES), _pad_last(_rot_cols(wkr), LANES)], axis=1).astype(BF16)
            wz = w[:, o3:].astype(BF16)[None]
            g = norm_mla[j].reshape(1, dm)
            qn, kvn = q_norm[j].reshape(1, ql), kv_norm[j].reshape(1, kvl)
            wqb = w_q_b[j]
            wn = wqb[:, :, :nope].reshape(ql, mh * nope).astype(BF16)
            wr = _pad_last(wqb[:, :, nope:], LANES).reshape(ql, mh * LANES).astype(BF16)
            wrr = _pad_last(_rot_cols(wqb[:, :, nope:]), LANES).reshape(ql, mh * LANES).astype(BF16)
            w_kv = jnp.stack([w_uk[j].reshape(kvl, mh * nope), w_uv[j].reshape(kvl, mh * vd)]).astype(BF16)
            w_ukt = jnp.transpose(w_uk[j], (1, 2, 0)).astype(BF16)
            w_uvh = jnp.transpose(w_uv[j], (1, 0, 2)).astype(BF16)
            w_out = w_out_mla[j].astype(BF16)

            z = _proj_stack(hp, g, wz, (1.0,), tm=1024, tn=1024, name="mla_z")
            qan, ckv, ckvb, kr, krp = _mla_a(hp, g, wqa, wkva, wkr2, qn, kvn, cos_p, sin_p, rope=rope, tm=512)
            qm, qx = _mla_q(qan, wn, wr, wrr, cos_p, sin_p, scale=mla_scale, tm=1024, hb=4)
            kvu = _proj_stack(ckvb, None, w_kv, (1.0, 1.0), tm=1024, tn=1024, name="mla_kv")
            o = _flash(qm.reshape(bsz, seq, mh * nope), qx.reshape(bsz, seq, mh * LANES),
                       kvu.reshape(2, bsz, seq, mh * nope), krp.reshape(bsz, seq, LANES),
                       kvu.reshape(2, bsz, seq, mh * vd), z.reshape(1, bsz, seq, mh * vd),
                       nkv=mh, group=1, tq=512, tk=512, qm_lead=None, km_lead=0, v_lead=1,
                       z_lead=0, ka_shared=True, name="mla_flash")
            o = o.reshape(mp, mh * vd)
            outs["mc_p"].append(ckv.reshape(bsz, seq, kvl))
            outs["mr_p"].append(kr.reshape(bsz, seq, rope))
            if last:
                yp = _out_final(o, None, 0, w_out, hp, gfin, tm=256)
            else:
                hp = _out_proj(o, None, 0, w_out, hp, tm=1024, tn=1024)

            z = _proj_stack(hs, g, wz, (1.0,), tm=1024, tn=512, name="mla_z_s")
            qan, ckv, ckvb, kr, krp = _mla_a(hs, g, wqa, wkva, wkr2, qn, kvn, cos_s, sin_s, rope=rope, tm=512)
            qm, qx = _mla_q(qan, wn, wr, wrr, cos_s, sin_s, scale=mla_scale, tm=1024, hb=4)
            qlat = _absorb(qm, w_ukt, tm=256)
            qx3 = qx.reshape(bd, nr, mh, LANES)[:, :ns].reshape(bd, ns * mh, LANES)
            olat = _mla_dec(page_table, qlat.reshape(kvl // LANES, bd, nr * mh, LANES), qx3,
                            ckvb, krp, cache_mla_ckv, ckrt, j, nh=mh, ns=ns)
            o = _unabsorb(olat.reshape(kvl // LANES, ms * mh, LANES), w_uvh, tm=128)
            outs["mc_s"].append(ckv.reshape(bd, nr, kvl)[:, :ns])
            outs["mr_s"].append(kr.reshape(bd, nr, rope)[:, :ns])
            if last:
                ys = _out_final(o, z, 0, w_out, hs, gfin, tm=256)
            else:
                hs = _out_proj(o, z, 0, w_out, hs, tm=512, tn=512)

    y_prompt = yp.reshape(bsz, seq, dm)
    y_sample = ys.reshape(bd, nr, dm)[:, :ns]
    st = lambda key: jnp.stack(outs[key])
    return (y_prompt, y_sample, st("fk_p"), st("fv_p"), st("fl_p"), st("fk_s"), st("fv_s"), st("fl_s"),
            st("mc_p"), st("mr_p"), st("mc_s"), st("mr_s"))
```

```python
import functools

import numpy as np
import jax
import jax.numpy as jnp
from jax import lax
from jax.experimental import pallas as pl
from jax.experimental.pallas import tpu as pltpu

F32 = jnp.float32
BF16 = jnp.bfloat16
EPS = 1e-6
ROPE_THETA = 10000.0
LOG2E = 1.4426950408889634
NEG = -1e30
LANES = 128
SUBLANES = 8
VMEM_LIMIT = 52 * 1024 * 1024
PAGES_PER_STEP = 16
MLA_PAGES_PER_STEP = 16
PAGE_SLOTS = 4


def _cparams(*sem):
    return pltpu.CompilerParams(dimension_semantics=sem, vmem_limit_bytes=VMEM_LIMIT)


def _rms(x, g):
    return x * lax.rsqrt(jnp.mean(x * x, axis=-1, keepdims=True) + EPS) * g


def _dot(a, b):
    return jnp.dot(a, b, preferred_element_type=F32)


def _dot_nt(a, b):
    return lax.dot_general(a, b, (((1,), (1,)), ((), ())), preferred_element_type=F32)


def _split3(x):
    hi = x.astype(BF16)
    r1 = x - hi.astype(F32)
    mid = r1.astype(BF16)
    lo = (r1 - mid.astype(F32)).astype(BF16)
    return hi, mid, lo


def _log_sigmoid(x):
    return jnp.minimum(x, 0.0) - jnp.log1p(jnp.exp(-jnp.abs(x)))


def _proj_kernel(*refs, norm, scales):
    if norm:
        x_ref, g_ref, w_ref, o_ref, h_ref = refs
    else:
        x_ref, w_ref, o_ref = refs
    gi = pl.program_id(1)
    if norm:
        @pl.when((gi == 0) & (pl.program_id(2) == 0))
        def _():
            h_ref[...] = _rms(x_ref[...], g_ref[...]).astype(BF16)
        h = h_ref[...]
    else:
        h = x_ref[...]
    acc = _dot(h, w_ref[...])
    if any(s != 1.0 for s in scales):
        sc = jnp.float32(scales[-1])
        for idx in range(len(scales) - 2, -1, -1):
            sc = jnp.where(gi == idx, jnp.float32(scales[idx]), sc)
        acc = acc * sc
    o_ref[...] = acc.astype(o_ref.dtype)


def _proj_stack(x, g, w, scales, *, tm, tn, name):
    m, kd = x.shape
    ng, _, n = w.shape
    norm = g is not None
    tm, tn = min(tm, m), min(tn, n)
    assert m % tm == 0 and n % tn == 0
    in_specs = [pl.BlockSpec((tm, kd), lambda i, k, j: (i, 0))]
    args = [x]
    scratch = []
    if norm:
        in_specs.append(pl.BlockSpec((1, kd), lambda i, k, j: (0, 0)))
        args.append(g)
        scratch.append(pltpu.VMEM((tm, kd), BF16))
    in_specs.append(pl.BlockSpec((None, kd, tn), lambda i, k, j: (k, 0, j)))
    args.append(w)
    return pl.pallas_call(
        functools.partial(_proj_kernel, norm=norm, scales=tuple(scales)),
        grid=(m // tm, ng, n // tn),
        in_specs=in_specs,
        out_specs=pl.BlockSpec((None, tm, tn), lambda i, k, j: (k, i, j)),
        out_shape=jax.ShapeDtypeStruct((ng, m, n), BF16),
        scratch_shapes=scratch,
        compiler_params=_cparams("parallel", "arbitrary", "arbitrary"),
        name=name,
    )(*args)


def _fox_kvf_kernel(x_ref, g_ref, wk_ref, wv_ref, wf_ref, bf_ref,
                    k_ref, v_ref, lf_ref, kb_ref, vb_ref, *, nh):
    h = _rms(x_ref[...], g_ref[...]).astype(BF16)
    k = _dot(h, wk_ref[...])
    v = _dot(h, wv_ref[...])
    tm = k.shape[0]
    nkv = k.shape[1] // LANES
    for g in range(nkv):
        k_ref[pl.ds(g, tm, stride=nkv), :] = k[:, g * LANES:(g + 1) * LANES]
        v_ref[pl.ds(g, tm, stride=nkv), :] = v[:, g * LANES:(g + 1) * LANES]
    kb_ref[...] = k.astype(BF16)
    vb_ref[...] = v.astype(BF16)
    lf = _log_sigmoid(_dot(h, wf_ref[...]) + bf_ref[...])
    lane = lax.broadcasted_iota(jnp.int32, lf.shape, 1)
    lf_ref[...] = jnp.where(lane < nh, lf, 0.0)


def _fox_kvf(x, g, wk, wv, wf, bf, *, nh, tm):
    m, kd = x.shape
    kvw = wk.shape[1]
    nkv = kvw // LANES
    tm = min(tm, m)
    assert m % tm == 0
    row = lambda i: (i, 0)
    fix = lambda i: (0, 0)
    return pl.pallas_call(
        functools.partial(_fox_kvf_kernel, nh=nh),
        grid=(m // tm,),
        in_specs=[pl.BlockSpec((tm, kd), row), pl.BlockSpec((1, kd), fix),
                  pl.BlockSpec((kd, kvw), fix), pl.BlockSpec((kd, kvw), fix),
                  pl.BlockSpec((kd, LANES), fix), pl.BlockSpec((1, LANES), fix)],
        out_specs=[pl.BlockSpec((tm * nkv, LANES), row), pl.BlockSpec((tm * nkv, LANES), row),
                   pl.BlockSpec((tm, LANES), row), pl.BlockSpec((tm, kvw), row),
                   pl.BlockSpec((tm, kvw), row)],
        out_shape=[jax.ShapeDtypeStruct((m * nkv, LANES), F32), jax.ShapeDtypeStruct((m * nkv, LANES), F32),
                   jax.ShapeDtypeStruct((m, LANES), F32), jax.ShapeDtypeStruct((m, kvw), BF16),
                   jax.ShapeDtypeStruct((m, kvw), BF16)],
        compiler_params=_cparams("parallel"),
        name="fox_kvf",
    )(x, g, wk, wv, wf, bf)


def _fox_aux_kernel(lf_ref, sq_ref, qc_ref, sk_ref, kc_ref, qa_ref, ka_ref, carry_ref):
    @pl.when(pl.program_id(1) == 0)
    def _():
        carry_ref[...] = jnp.zeros_like(carry_ref)
    lf = lf_ref[...]
    tc = lf.shape[0]
    r = lax.broadcasted_iota(jnp.int32, (tc, tc), 0)
    c = lax.broadcasted_iota(jnp.int32, (tc, tc), 1)
    tri = jnp.where(r >= c, 1.0, 0.0).astype(BF16)
    p = _split3(lf)
    cc = carry_ref[...] + (_dot(tri, p[0]) + _dot(tri, p[1]) + _dot(tri, p[2]))
    carry_ref[...] = cc[tc - 1:tc, :]
    p2 = _split3(cc * LOG2E)
    qa = qc_ref[...] + (_dot(p2[0], sq_ref[0]) + _dot(p2[1], sq_ref[1]) + _dot(p2[2], sq_ref[2]))
    ka = kc_ref[...] + (_dot(p2[0], sk_ref[0]) + _dot(p2[1], sk_ref[1]) + _dot(p2[2], sk_ref[2]))
    qa_ref[...] = qa.astype(BF16)
    ka_ref[...] = ka.astype(BF16)


def _fox_aux_tables(nh, nkv):
    grp = nh // nkv
    sq = np.zeros((3, LANES, nh * LANES), np.float32)
    qc = np.zeros((1, nh * LANES), np.float32)
    sk = np.zeros((3, LANES, nkv * LANES), np.float32)
    kc = np.zeros((1, nkv * LANES), np.float32)
    for h in range(nh):
        g, j = divmod(h, grp)
        for piece in range(3):
            sq[piece, h, h * LANES + piece] = 1.0
            qc[0, h * LANES + 3 + 3 * j + piece] = 1.0
            sk[piece, h, g * LANES + 3 + 3 * j + piece] = -1.0
    for g in range(nkv):
        kc[0, g * LANES:g * LANES + 3] = 1.0
    return (jnp.asarray(sq, BF16), jnp.asarray(qc, F32), jnp.asarray(sk, BF16), jnp.asarray(kc, F32))


def _fox_aux(lf, tables, *, nh, nkv, tc):
    b, s, _ = lf.shape
    sq, qc, sk, kc = tables
    tc = min(tc, s)
    assert s % tc == 0
    return pl.pallas_call(
        _fox_aux_kernel,
        grid=(b, s // tc),
        in_specs=[pl.BlockSpec((None, tc, LANES), lambda bi, i: (bi, i, 0)),
                  pl.BlockSpec(sq.shape, lambda bi, i: (0, 0, 0)),
                  pl.BlockSpec(qc.shape, lambda bi, i: (0, 0)),
                  pl.BlockSpec(sk.shape, lambda bi, i: (0, 0, 0)),
                  pl.BlockSpec(kc.shape, lambda bi, i: (0, 0))],
        out_specs=[pl.BlockSpec((None, tc, nh * LANES), lambda bi, i: (bi, i, 0)),
                   pl.BlockSpec((None, tc, nkv * LANES), lambda bi, i: (bi, i, 0))],
        out_shape=[jax.ShapeDtypeStruct((b, s, nh * LANES), BF16),
                   jax.ShapeDtypeStruct((b, s, nkv * LANES), BF16)],
        scratch_shapes=[pltpu.VMEM((1, LANES), F32)],
        compiler_params=_cparams("parallel", "arbitrary"),
        name="fox_aux",
    )(lf, sq, qc, sk, kc)


def _silu(z):
    return z * (1.0 / (1.0 + jnp.exp(-z)))


def _flash_kernel(qm_ref, qa_ref, km_ref, ka_ref, v_ref, z_ref, o_ref,
                  q_scr, s0_scr, s1_scr, m_scr, acc_scr, *, group, tq, tk, hd):
    i = pl.program_id(2)
    rows = group * tq
    per = tq // tk
    s_bufs = (s0_scr, s1_scr)
    for j in range(group):
        q_scr[j * tq:(j + 1) * tq, 0:hd] = qm_ref[:, j * hd:(j + 1) * hd]
        q_scr[j * tq:(j + 1) * tq, hd:2 * hd] = qa_ref[:, j * hd:(j + 1) * hd]
    m_scr[...] = jnp.full_like(m_scr, NEG)
    acc_scr[...] = jnp.zeros_like(acc_scr)
    ones = jnp.ones((tk, hd), BF16)

    def scores(step, slot, mask_off):
        ks = pl.multiple_of(step * tk, tk)
        k = jnp.concatenate([km_ref[pl.ds(ks, tk), :], ka_ref[pl.ds(ks, tk), :]], axis=1)
        s = _dot_nt(q_scr[...], k)
        if mask_off is not None:
            t = lax.broadcasted_iota(jnp.int32, (rows, tk), 0) & (tq - 1)
            col = lax.broadcasted_iota(jnp.int32, (rows, tk), 1) + mask_off
            s = jnp.where(col <= t, s, NEG)
        s_bufs[slot][...] = s

    def consume(step, slot):
        ks = pl.multiple_of(step * tk, tk)
        s_ref = s_bufs[slot]
        m_prev = m_scr[...]
        m_new = jnp.maximum(m_prev, jnp.max(s_ref[...], axis=1, keepdims=True))
        alpha = jnp.exp2(m_prev - m_new)
        p = jnp.exp2(s_ref[...] - jnp.tile(m_new, (1, tk // hd)))
        v1 = jnp.concatenate([v_ref[pl.ds(ks, tk), :], ones], axis=1)
        acc_scr[...] = jnp.tile(alpha, (1, 2)) * acc_scr[...] + _dot(p.astype(BF16), v1)
        m_scr[...] = m_new

    def tail(first, n_plain):
        seq = [(first + n, None) for n in range(n_plain)]
        seq += [(first + n_plain + d, d * tk) for d in range(per)]
        for n, (step, _) in enumerate(seq):
            if n + 1 < len(seq):
                scores(seq[n + 1][0], (n + 1) % 2, seq[n + 1][1])
            consume(step, n % 2)

    n_plain = i * per

    @pl.when(i == 0)
    def _():
        scores(0, 0, 0)
        tail(0, 0)

    @pl.when(i > 0)
    def _():
        scores(0, 0, None)
        n_pairs = (n_plain - 1) // 2

        def body(jj, carry):
            scores(2 * jj + 1, 1, None)
            consume(2 * jj, 0)
            scores(2 * jj + 2, 0, None)
            consume(2 * jj + 1, 1)
            return carry

        lax.fori_loop(0, n_pairs, body, 0)
        if per % 2 == 0:
            tail(2 * n_pairs, 2)
        else:
            @pl.when(n_plain % 2 == 1)
            def _():
                tail(2 * n_pairs, 1)

            @pl.when(n_plain % 2 == 0)
            def _():
                tail(2 * n_pairs, 2)

    acc = acc_scr[...]
    o = acc[:, :hd] / acc[:, hd:]
    for j in range(group):
        gate = _silu(z_ref[:, j * hd:(j + 1) * hd].astype(F32))
        o_ref[:, j * hd:(j + 1) * hd] = (o[j * tq:(j + 1) * tq, :] * gate).astype(o_ref.dtype)


def _flash(qm, qa, km, ka, v, z, *, nkv, group, tq, tk, qm_lead, km_lead, v_lead, z_lead, ka_shared, name):
    hd = LANES
    b, s = qa.shape[0], qa.shape[1]
    tq, tk = min(tq, s), min(tk, s)
    assert s % tq == 0 and tq % tk == 0

    def spec(lead, blk_rows, width, imap):
        if lead is None:
            return pl.BlockSpec((None, blk_rows, width), imap)
        return pl.BlockSpec((None, None, blk_rows, width), lambda bi, g, i: (lead,) + imap(bi, g, i))

    q_map = lambda bi, g, i: (bi, i, g)
    k_map = lambda bi, g, i: (bi, 0, g)
    ka_map = (lambda bi, g, i: (bi, 0, 0)) if ka_shared else k_map
    return pl.pallas_call(
        functools.partial(_flash_kernel, group=group, tq=tq, tk=tk, hd=hd),
        grid=(b, nkv, s // tq),
        in_specs=[spec(qm_lead, tq, group * hd, q_map), spec(None, tq, group * hd, q_map),
                  spec(km_lead, s, hd, k_map), spec(None, s, hd, ka_map), spec(v_lead, s, hd, k_map),
                  spec(z_lead, tq, group * hd, q_map)],
        out_specs=pl.BlockSpec((None, tq, group * hd), q_map),
        out_shape=jax.ShapeDtypeStruct((b, s, nkv * group * hd), BF16),
        scratch_shapes=[pltpu.VMEM((group * tq, 2 * hd), BF16), pltpu.VMEM((group * tq, tk), F32),
                        pltpu.VMEM((group * tq, tk), F32), pltpu.VMEM((group * tq, hd), F32),
                        pltpu.VMEM((group * tq, 2 * hd), F32)],
        compiler_params=_cparams("parallel", "parallel", "arbitrary"),
        name=name,
    )(qm, qa, km, ka, v, z)


def _gated(o_ref, z_ref):
    if z_ref is None:
        return o_ref[...]
    return (o_ref[...].astype(F32) * _silu(z_ref[...].astype(F32))).astype(BF16)


def _out_proj_kernel(*refs, gated):
    if gated:
        o_ref, z_ref, w_ref, r_ref, y_ref, u_scr = refs

        @pl.when(pl.program_id(1) == 0)
        def _():
            u_scr[...] = _gated(o_ref, z_ref)
        u = u_scr[...]
    else:
        o_ref, w_ref, r_ref, y_ref = refs
        u = o_ref[...]
    y_ref[...] = r_ref[...] + _dot(u, w_ref[...])


def _out_proj(o, z, z_lead, w, resid, *, tm, tn):
    m, kd = o.shape
    n = w.shape[1]
    tm, tn = min(tm, m), min(tn, n)
    assert m % tm == 0 and n % tn == 0
    gated = z is not None
    in_specs = [pl.BlockSpec((tm, kd), lambda i, j: (i, 0))]
    args = [o]
    if gated:
        in_specs.append(pl.BlockSpec((None, tm, kd), lambda i, j: (z_lead, i, 0)))
        args.append(z)
    in_specs += [pl.BlockSpec((kd, tn), lambda i, j: (0, j)), pl.BlockSpec((tm, tn), lambda i, j: (i, j))]
    return pl.pallas_call(
        functools.partial(_out_proj_kernel, gated=gated),
        grid=(m // tm, n // tn),
        in_specs=in_specs,
        out_specs=pl.BlockSpec((tm, tn), lambda i, j: (i, j)),
        out_shape=jax.ShapeDtypeStruct((m, n), F32),
        scratch_shapes=[pltpu.VMEM((tm, kd), BF16)] if gated else [],
        compiler_params=_cparams("parallel", "arbitrary"),
        name="out_proj",
    )(*args, w, resid)


def _out_final_kernel(*refs, gated):
    if gated:
        o_ref, z_ref, w_ref, r_ref, g_ref, y_ref = refs
    else:
        o_ref, w_ref, r_ref, g_ref, y_ref = refs
        z_ref = None
    hsum = r_ref[...] + _dot(_gated(o_ref, z_ref), w_ref[...])
    y_ref[...] = _rms(hsum, g_ref[...])


def _out_final(o, z, z_lead, w, resid, gfin, *, tm):
    m, kd = o.shape
    n = w.shape[1]
    tm = min(tm, m)
    assert m % tm == 0
    gated = z is not None
    in_specs = [pl.BlockSpec((tm, kd), lambda i: (i, 0))]
    args = [o]
    if gated:
        in_specs.append(pl.BlockSpec((None, tm, kd), lambda i: (z_lead, i, 0)))
        args.append(z)
    in_specs += [pl.BlockSpec((kd, n), lambda i: (0, 0)), pl.BlockSpec((tm, n), lambda i: (i, 0)),
                 pl.BlockSpec((1, n), lambda i: (0, 0))]
    return pl.pallas_call(
        functools.partial(_out_final_kernel, gated=gated),
        grid=(m // tm,),
        in_specs=in_specs,
        out_specs=pl.BlockSpec((tm, n), lambda i: (i, 0)),
        out_shape=jax.ShapeDtypeStruct((m, n), F32),
        compiler_params=_cparams("parallel"),
        name="out_final",
    )(*args, w, resid, gfin)


def _mla_a_kernel(x_ref, g_ref, wqa_ref, wkva_ref, wkr_ref, qn_ref, kvn_ref, cos_ref, sin_ref,
                  qa_ref, ckv_ref, ckvb_ref, kr_ref, krp_ref, *, rope):
    h = _rms(x_ref[...], g_ref[...]).astype(BF16)
    qa_ref[...] = _rms(_dot(h, wqa_ref[...]), qn_ref[...]).astype(BF16)
    ckv = _rms(_dot(h, wkva_ref[...]), kvn_ref[...])
    ckv_ref[...] = ckv
    ckvb_ref[...] = ckv.astype(BF16)
    kr2 = _dot(h, wkr_ref[...])
    krp = kr2[:, :LANES] * cos_ref[...] + kr2[:, LANES:] * sin_ref[...]
    kr_ref[...] = krp[:, :rope]
    krp_ref[...] = krp.astype(BF16)


def _mla_a(x, g, wqa, wkva, wkr2, qn, kvn, cos, sin, *, rope, tm):
    m, kd = x.shape
    ql, kvl = wqa.shape[1], wkva.shape[1]
    tm = min(tm, m)
    assert m % tm == 0
    nt = cos.shape[0] // tm
    row = lambda i: (i, 0)
    fix = lambda i: (0, 0)
    tab = lambda i: (i % nt, 0)
    return pl.pallas_call(
        functools.partial(_mla_a_kernel, rope=rope),
        grid=(m // tm,),
        in_specs=[pl.BlockSpec((tm, kd), row), pl.BlockSpec((1, kd), fix),
                  pl.BlockSpec((kd, ql), fix), pl.BlockSpec((kd, kvl), fix),
                  pl.BlockSpec((kd, 2 * LANES), fix), pl.BlockSpec((1, ql), fix),
                  pl.BlockSpec((1, kvl), fix), pl.BlockSpec((tm, LANES), tab),
                  pl.BlockSpec((tm, LANES), tab)],
        out_specs=[pl.BlockSpec((tm, ql), row), pl.BlockSpec((tm, kvl), row),
                   pl.BlockSpec((tm, kvl), row), pl.BlockSpec((tm, rope), row),
                   pl.BlockSpec((tm, LANES), row)],
        out_shape=[jax.ShapeDtypeStruct((m, ql), BF16), jax.ShapeDtypeStruct((m, kvl), F32),
                   jax.ShapeDtypeStruct((m, kvl), BF16), jax.ShapeDtypeStruct((m, rope), F32),
                   jax.ShapeDtypeStruct((m, LANES), BF16)],
        compiler_params=_cparams("parallel"),
        name="mla_a",
    )(x, g, wqa, wkva, wkr2, qn, kvn, cos, sin)


def _mla_q_kernel(x_ref, wn_ref, wr_ref, wrr_ref, cos_ref, sin_ref, qm_ref, qa_ref, *, hb, scale):
    x = x_ref[...]
    qm_ref[...] = (_dot(x, wn_ref[...]) * scale).astype(BF16)
    qr = _dot(x, wr_ref[...])
    qrr = _dot(x, wrr_ref[...])
    cos = cos_ref[...]
    sin = sin_ref[...]
    for jj in range(hb):
        sl = slice(jj * LANES, (jj + 1) * LANES)
        qa_ref[:, sl] = ((qr[:, sl] * cos + qrr[:, sl] * sin) * scale).astype(BF16)


def _mla_q(x, wn, wr, wrr, cos, sin, *, scale, tm, hb):
    m, kd = x.shape
    n = wn.shape[1]
    tm = min(tm, m)
    assert m % tm == 0
    tn = hb * LANES
    nt = cos.shape[0] // tm
    wmap = lambda i, j: (0, j)
    omap = lambda i, j: (i, j)
    tab = lambda i, j: (i % nt, 0)
    return pl.pallas_call(
        functools.partial(_mla_q_kernel, hb=hb, scale=scale),
        grid=(m // tm, n // tn),
        in_specs=[pl.BlockSpec((tm, kd), lambda i, j: (i, 0)),
                  pl.BlockSpec((kd, tn), wmap), pl.BlockSpec((kd, tn), wmap), pl.BlockSpec((kd, tn), wmap),
                  pl.BlockSpec((tm, LANES), tab), pl.BlockSpec((tm, LANES), tab)],
        out_specs=[pl.BlockSpec((tm, tn), omap), pl.BlockSpec((tm, tn), omap)],
        out_shape=[jax.ShapeDtypeStruct((m, n), BF16), jax.ShapeDtypeStruct((m, n), BF16)],
        compiler_params=_cparams("parallel", "arbitrary"),
        name="mla_q",
    )(x, wn, wr, wrr, cos, sin)


def _absorb_kernel(x_ref, w_ref, o_ref, *, nh):
    tm = x_ref.shape[0]
    kd = x_ref.shape[1] // nh
    for h in range(nh):
        y = _dot(x_ref[:, h * kd:(h + 1) * kd], w_ref[h])
        for c in range(o_ref.shape[0]):
            o_ref[c, pl.ds(h, tm, stride=nh), :] = y[:, c * LANES:(c + 1) * LANES]


def _absorb(x, w, *, tm):
    m = x.shape[0]
    nh, kd, n = w.shape
    tm = min(tm, m)
    assert m % tm == 0
    return pl.pallas_call(
        functools.partial(_absorb_kernel, nh=nh),
        grid=(m // tm,),
        in_specs=[pl.BlockSpec((tm, nh * kd), lambda i: (i, 0)), pl.BlockSpec((nh, kd, n), lambda i: (0, 0, 0))],
        out_specs=pl.BlockSpec((n // LANES, tm * nh, LANES), lambda i: (0, i, 0)),
        out_shape=jax.ShapeDtypeStruct((n // LANES, m * nh, LANES), F32),
        compiler_params=_cparams("parallel"),
        name="mla_absorb",
    )(x, w)


def _unabsorb_kernel(x_ref, w_ref, o_ref, *, nh):
    tm = o_ref.shape[0]
    n = o_ref.shape[1] // nh
    for h in range(nh):
        xh = jnp.concatenate([x_ref[c, pl.ds(h, tm, stride=nh), :] for c in range(x_ref.shape[0])], axis=1)
        o_ref[:, h * n:(h + 1) * n] = _dot(xh.astype(BF16), w_ref[h]).astype(o_ref.dtype)


def _unabsorb(x, w, *, tm):
    nh, kd, n = w.shape
    m = x.shape[1] // nh
    tm = min(tm, m)
    assert m % tm == 0
    return pl.pallas_call(
        functools.partial(_unabsorb_kernel, nh=nh),
        grid=(m // tm,),
        in_specs=[pl.BlockSpec((kd // LANES, tm * nh, LANES), lambda i: (0, i, 0)),
                  pl.BlockSpec((nh, kd, n), lambda i: (0, 0, 0))],
        out_specs=pl.BlockSpec((tm, nh * n), lambda i: (i, 0)),
        out_shape=jax.ShapeDtypeStruct((m, nh * n), BF16),
        compiler_params=_cparams("parallel"),
        name="mla_uv",
    )(x, w)


def _softmax_step(s, v, m_scr, l_scr, acc_scr, idx):
    m_prev = m_scr[idx]
    m_new = jnp.maximum(m_prev, jnp.max(s, axis=1, keepdims=True))
    alpha = jnp.exp2(m_prev - m_new)
    p = jnp.exp2(s - m_new)
    l_scr[idx] = alpha * l_scr[idx] + jnp.sum(p, axis=1, keepdims=True)
    acc_scr[idx] = alpha * acc_scr[idx] + _dot(p.astype(BF16), v)
    m_scr[idx] = m_new


def _pad_rows(x, rows):
    return jnp.concatenate([x, jnp.zeros((rows - x.shape[0], x.shape[1]), x.dtype)], axis=0)


def _page_copies(tab_ref, seq, first_page, layer, srcs, bufs, sems, slot, npg):
    copies = []
    for p in range(npg):
        pg = tab_ref[seq, first_page + p]
        for a, (src, buf) in enumerate(zip(srcs, bufs)):
            copies.append(pltpu.make_async_copy(src.at[layer, pg], buf.at[slot, p], sems.at[a, slot]))
    return copies


def _paged_pipeline(tab_ref, srcs, bufs, sems, *, layer, npg, nc, reverse, on_chunk):
    b, c2 = pl.program_id(0), pl.program_id(1)
    total = pl.num_programs(0) * nc
    n0 = (b * (nc // PAGE_SLOTS) + c2) * PAGE_SLOTS

    def copies(n, slot):
        seq, cc = n // nc, n % nc
        first = ((nc - 1 - cc) if reverse else cc) * npg
        return _page_copies(tab_ref, seq, first, layer, srcs, bufs, sems, slot, npg)

    @pl.when(n0 == 0)
    def _():
        for slot in range(PAGE_SLOTS):
            for cp in copies(slot, slot):
                cp.start()

    for slot in range(PAGE_SLOTS):
        n = n0 + slot
        for cp in copies(n, slot):
            cp.wait()
        on_chunk(slot)

        @pl.when(n + PAGE_SLOTS < total)
        def _():
            for cp in copies(n + PAGE_SLOTS, slot):
                cp.start()


def _fox_dec_kernel(tab_ref, q_ref, lfn_ref, kn_ref, vn_ref, ck_ref, cv_ref, clf_ref, o_ref,
                    kbuf, vbuf, lbuf, sems, m_scr, l_scr, acc_scr, car_scr, cn_scr,
                    *, layer, npg, nc, nkv, grp, page):
    c2 = pl.program_id(1)
    hd = LANES
    nr = SUBLANES
    grows = grp * nr

    def q_pair(pr):
        blocks = []
        for gg in range(2):
            g = 2 * pr + gg
            qg = jnp.concatenate([q_ref[:, (g * grp + j) * hd:(g * grp + j + 1) * hd] for j in range(grp)], axis=0)
            zero = jnp.zeros_like(qg)
            blocks.append(jnp.concatenate([qg, zero] if gg == 0 else [zero, qg], axis=1))
        return jnp.concatenate(blocks, axis=0)

    def biased(s, pr, col_bias, row_bias, sign, mask):
        parts = []
        for gg in range(2):
            for j in range(grp):
                h = (2 * pr + gg) * grp + j
                r0 = gg * grows + j * nr
                sj = s[r0:r0 + nr, :] + col_bias[:, h:h + 1] + sign * row_bias[h:h + 1, :]
                parts.append(sj if mask is None else jnp.where(mask, sj, NEG))
        return jnp.concatenate(parts, axis=0)

    @pl.when(c2 == 0)
    def _():
        m_scr[...] = jnp.full_like(m_scr, NEG)
        l_scr[...] = jnp.zeros_like(l_scr)
        acc_scr[...] = jnp.zeros_like(acc_scr)
        car_scr[...] = jnp.zeros_like(car_scr)
        lf = lfn_ref[...]
        rid = lax.broadcasted_iota(jnp.int32, lf.shape, 0)
        cn = jnp.zeros_like(lf)
        for t in range(nr):
            cn = cn + jnp.where(rid >= t, lf[t:t + 1, :], 0.0)
        cn = cn * LOG2E
        cn_scr[...] = cn
        cnt = _pad_rows(cn, LANES).T
        mask = (lax.broadcasted_iota(jnp.int32, (nr, LANES), 1)
                <= lax.broadcasted_iota(jnp.int32, (nr, LANES), 0))
        for pr in range(nkv // 2):
            kk = _pad_rows(kn_ref[:, 2 * pr * hd:(2 * pr + 2) * hd], LANES)
            vv = _pad_rows(vn_ref[:, 2 * pr * hd:(2 * pr + 2) * hd], LANES)
            s = biased(_dot_nt(q_pair(pr), kk), pr, cn, cnt, -1.0, mask)
            _softmax_step(s, vv, m_scr, l_scr, acc_scr, pr)

    def on_chunk(slot):
        lfs = [lbuf[slot, p] for p in range(npg)]
        pieces = [_split3(x) for x in lfs]
        stacked = jnp.concatenate([pieces[p][k] for k in range(3) for p in range(npg)], axis=0)
        r = lax.broadcasted_iota(jnp.int32, (page, page), 0)
        cc = lax.broadcasted_iota(jnp.int32, (page, page), 1)
        upper = jnp.where(r > cc, 1.0, 0.0).astype(BF16)
        res = _dot(stacked, upper)
        nh = lfs[0].shape[0]
        off = car_scr[...]
        dks = [None] * npg
        for p in range(npg - 1, -1, -1):
            loc = (res[p * nh:(p + 1) * nh] + res[(npg + p) * nh:(npg + p + 1) * nh]
                   + res[(2 * npg + p) * nh:(2 * npg + p + 1) * nh])
            dks[p] = (loc + off) * LOG2E
            off = off + (loc[:, 0:1] + lfs[p][:, 0:1])
        car_scr[...] = off
        dk = jnp.concatenate(dks, axis=1)
        cn = cn_scr[...]

        def cached_pair(buf, pr):
            return jnp.concatenate(
                [jnp.concatenate([buf[slot, p, pl.ds(2 * pr + gg, page, stride=nkv), :] for gg in range(2)], axis=1)
                 for p in range(npg)], axis=0).astype(BF16)

        for pr in range(nkv // 2):
            s = biased(_dot_nt(q_pair(pr), cached_pair(kbuf, pr)), pr, cn, dk, 1.0, None)
            _softmax_step(s, cached_pair(vbuf, pr), m_scr, l_scr, acc_scr, pr)

    _paged_pipeline(tab_ref, (ck_ref, cv_ref, clf_ref), (kbuf, vbuf, lbuf), sems,
                    layer=layer, npg=npg, nc=nc, reverse=True, on_chunk=on_chunk)

    @pl.when(c2 == pl.num_programs(1) - 1)
    def _():
        for pr in range(nkv // 2):
            o = acc_scr[pr] / l_scr[pr]
            for gg in range(2):
                for j in range(grp):
                    h = (2 * pr + gg) * grp + j
                    r0 = gg * grows + j * nr
                    o_ref[:, h * hd:(h + 1) * hd] = o[r0:r0 + nr, gg * hd:(gg + 1) * hd].astype(o_ref.dtype)


def _fox_dec(page_table, q, q_lead, lfn, kn, vn, ck, cv, clf, layer, *, nh, nkv):
    bd, n_pages = page_table.shape
    npg = min(PAGES_PER_STEP, n_pages)
    nc = n_pages // npg
    assert nc % PAGE_SLOTS == 0 and nkv % 2 == 0
    page = clf.shape[-1]
    hd = LANES
    nr = SUBLANES
    grp = nh // nkv
    row = lambda b, c, tab: (b, 0)
    hbm = pl.BlockSpec(memory_space=pl.ANY)
    in_specs = [pl.BlockSpec((None, nr, nh * hd), lambda b, c, tab: (q_lead, b, 0)),
                pl.BlockSpec((nr, LANES), row), pl.BlockSpec((nr, nkv * hd), row),
                pl.BlockSpec((nr, nkv * hd), row), hbm, hbm, hbm]
    rows = 2 * grp * nr
    return pl.pallas_call(
        functools.partial(_fox_dec_kernel, layer=layer, npg=npg, nc=nc, nkv=nkv, grp=grp, page=page),
        grid_spec=pltpu.PrefetchScalarGridSpec(
            num_scalar_prefetch=1, grid=(bd, nc // PAGE_SLOTS), in_specs=in_specs,
            out_specs=pl.BlockSpec((nr, nh * hd), row),
            scratch_shapes=[pltpu.VMEM((PAGE_SLOTS, npg, page * nkv, hd), F32), pltpu.VMEM((PAGE_SLOTS, npg, page * nkv, hd), F32),
                            pltpu.VMEM((PAGE_SLOTS, npg, nh, page), F32), pltpu.SemaphoreType.DMA((3, PAGE_SLOTS)),
                            pltpu.VMEM((nkv // 2, rows, 1), F32), pltpu.VMEM((nkv // 2, rows, 1), F32),
                            pltpu.VMEM((nkv // 2, rows, 2 * hd), F32), pltpu.VMEM((nh, 1), F32),
                            pltpu.VMEM((nr, LANES), F32)]),
        out_shape=jax.ShapeDtypeStruct((bd * nr, nh * hd), BF16),
        compiler_params=_cparams("arbitrary", "arbitrary"),
        name="fox_dec",
    )(page_table, q, lfn, kn, vn, ck, cv, clf)


def _mla_dec_kernel(tab_ref, ql_ref, qa_ref, cn_ref, krn_ref, cc_ref, cr_ref, o_ref,
                    cbuf, rbuf, sems, m_scr, l_scr, acc_scr, *, layer, npg, nc, nh, ns, rope):
    c2 = pl.program_id(1)
    rows = ns * nh
    nslab = ql_ref.shape[0]
    ql = jnp.concatenate([ql_ref[c] for c in range(nslab)], axis=1).astype(BF16)
    qa = qa_ref[...]

    @pl.when(c2 == 0)
    def _():
        m_scr[...] = jnp.full_like(m_scr, NEG)
        l_scr[...] = jnp.zeros_like(l_scr)
        acc_scr[...] = jnp.zeros_like(acc_scr)
        cnew = _pad_rows(cn_ref[...], LANES)
        s = _dot_nt(ql, cnew) + _dot_nt(qa, _pad_rows(krn_ref[...], LANES))
        t_id = lax.broadcasted_iota(jnp.int32, (ns, nh, LANES), 0).reshape(rows, LANES)
        s_id = lax.broadcasted_iota(jnp.int32, (rows, LANES), 1)
        _softmax_step(jnp.where(s_id <= t_id, s, NEG), cnew, m_scr, l_scr, acc_scr, 0)

    def on_chunk(slot):
        ckv = jnp.concatenate([cbuf[slot, p] for p in range(npg)], axis=0).astype(BF16)
        krt = jnp.concatenate([rbuf[slot, p] for p in range(npg)], axis=1).astype(BF16)
        s = _dot_nt(ql, ckv) + _dot(qa[:, :rope], krt)
        _softmax_step(s, ckv, m_scr, l_scr, acc_scr, 0)

    _paged_pipeline(tab_ref, (cc_ref, cr_ref), (cbuf, rbuf), sems,
                    layer=layer, npg=npg, nc=nc, reverse=False, on_chunk=on_chunk)

    @pl.when(c2 == pl.num_programs(1) - 1)
    def _():
        o = acc_scr[0] / l_scr[0]
        for c in range(nslab):
            o_ref[c, 0:rows, :] = o[:, c * LANES:(c + 1) * LANES]
            o_ref[c, rows:, :] = jnp.zeros((o_ref.shape[1] - rows, LANES), o_ref.dtype)


def _mla_dec(page_table, qlat, qaux, cnew, krnew, cckv, ckrt, layer, *, nh, ns):
    bd, n_pages = page_table.shape
    npg = min(MLA_PAGES_PER_STEP, n_pages)
    nc = n_pages // npg
    assert nc % PAGE_SLOTS == 0
    page, kvl = cckv.shape[-2], cckv.shape[-1]
    rope = ckrt.shape[-2]
    nr = SUBLANES
    rows = ns * nh
    row = lambda b, c, tab: (b, 0)
    seq = lambda b, c, tab: (b, 0, 0)
    hbm = pl.BlockSpec(memory_space=pl.ANY)
    nslab = kvl // LANES
    slab = lambda b, c, tab: (0, b, 0, 0)
    in_specs = [pl.BlockSpec((nslab, None, rows, LANES), slab), pl.BlockSpec((None, rows, LANES), seq),
                pl.BlockSpec((nr, kvl), row), pl.BlockSpec((nr, LANES), row), hbm, hbm]
    return pl.pallas_call(
        functools.partial(_mla_dec_kernel, layer=layer, npg=npg, nc=nc, nh=nh, ns=ns, rope=rope),
        grid_spec=pltpu.PrefetchScalarGridSpec(
            num_scalar_prefetch=1, grid=(bd, nc // PAGE_SLOTS), in_specs=in_specs,
            out_specs=pl.BlockSpec((nslab, None, nr * nh, LANES), slab),
            scratch_shapes=[pltpu.VMEM((PAGE_SLOTS, npg, page, kvl), F32), pltpu.VMEM((PAGE_SLOTS, npg, rope, page), F32),
                            pltpu.SemaphoreType.DMA((2, PAGE_SLOTS)),
                            pltpu.VMEM((1, rows, 1), F32), pltpu.VMEM((1, rows, 1), F32),
                            pltpu.VMEM((1, rows, kvl), F32)]),
        out_shape=jax.ShapeDtypeStruct((nslab, bd, nr * nh, LANES), F32),
        compiler_params=_cparams("arbitrary", "arbitrary"),
        name="mla_dec",
    )(page_table, qlat, qaux, cnew, krnew, cckv, ckrt)


def _rope_tables(pos, rope):
    half = rope // 2
    inv = 1.0 / (ROPE_THETA ** (jnp.arange(half, dtype=F32) * (2.0 / rope)))
    ang = pos.astype(F32)[:, None] * inv[None, :]
    zeros = jnp.zeros((pos.shape[0], LANES - rope), F32)
    cos = jnp.concatenate([jnp.cos(ang), jnp.cos(ang), zeros], axis=1)
    sin = jnp.concatenate([jnp.sin(ang), jnp.sin(ang), zeros], axis=1)
    return cos, sin


def _rot_cols(w):
    half = w.shape[-1] // 2
    return jnp.concatenate([-w[..., half:], w[..., :half]], axis=-1)


def _pad_last(w, width):
    pad = [(0, 0)] * (w.ndim - 1) + [(0, width - w.shape[-1])]
    return jnp.pad(w, pad)


def kernel(x_prompt, x_sample, cache_fox_k, cache_fox_v, cache_fox_logf, cache_mla_ckv, cache_mla_krope, page_table, norm_fox, w_in_fox, b_f, w_out_fox, norm_mla, w_in_mla, q_norm, w_q_b, kv_norm, w_uk, w_uv, w_out_mla, final_norm):
    bsz, seq, dm = x_prompt.shape
    bd, ns, _ = x_sample.shape
    nf, n_pool, page, nkv, hd = cache_fox_k.shape
    nm = cache_mla_ckv.shape[0]
    nh = b_f.shape[1]
    grp = nh // nkv
    fw, kvw = nh * hd, nkv * hd
    n_pages = page_table.shape[1]
    past = n_pages * page
    _, ql, mh, qk = w_q_b.shape
    kvl, nope = w_uk.shape[1], w_uk.shape[3]
    rope = qk - nope
    vd = w_uv.shape[3]
    assert hd == LANES and nope == LANES and vd == LANES and rope <= LANES and ns <= SUBLANES
    nr = SUBLANES
    mp, ms = bsz * seq, bd * nr

    hp = x_prompt.reshape(mp, dm)
    hs = jnp.pad(x_sample, ((0, 0), (0, nr - ns), (0, 0))).reshape(ms, dm)

    ck = cache_fox_k.reshape(nf, n_pool, page * nkv, hd)
    cv = cache_fox_v.reshape(nf, n_pool, page * nkv, hd)
    clf = jnp.swapaxes(cache_fox_logf, 2, 3)
    ckrt = jnp.swapaxes(cache_mla_krope, 2, 3)

    cos_p, sin_p = _rope_tables(jnp.arange(seq), rope)
    cos_s, sin_s = _rope_tables(past + jnp.arange(nr), rope)
    cos_s, sin_s = jnp.tile(cos_s, (bd, 1)), jnp.tile(sin_s, (bd, 1))
    aux_tables = _fox_aux_tables(nh, nkv)
    fox_scale = hd ** -0.5 * LOG2E
    mla_scale = qk ** -0.5 * LOG2E
    gfin = final_norm.reshape(1, dm)

    outs = {k: [] for k in ("fk_p", "fv_p", "fl_p", "fk_s", "fv_s", "fl_s", "mc_p", "mr_p", "mc_s", "mr_s")}
    depth = nf + nm
    yp = ys = None
    for i in range(depth):
        j = i // 2
        last = i == depth - 1
        if i % 2 == 0:
            w = w_in_fox[j]
            o1, o2, o3, o4 = fw, fw + kvw, fw + 2 * kvw, fw + 2 * kvw + nh
            w_qz = jnp.stack([w[:, :o1], w[:, o4:]]).astype(BF16)
            wk, wv = w[:, o1:o2].astype(BF16), w[:, o2:o3].astype(BF16)
            wf = _pad_last(w[:, o3:o4], LANES).astype(BF16)
            bfp = _pad_last(b_f[j].reshape(1, nh), LANES)
            g = norm_fox[j].reshape(1, dm)
            w_out = w_out_fox[j].astype(BF16)

            qz = _proj_stack(hp, g, w_qz, (fox_scale, 1.0), tm=1024, tn=1024, name="fox_qz")
            k, v, lf, kb, vb = _fox_kvf(hp, g, wk, wv, wf, bfp, nh=nh, tm=512)
            qa, ka = _fox_aux(lf.reshape(bsz, seq, LANES), aux_tables, nh=nh, nkv=nkv, tc=512)
            qz4 = qz.reshape(2, bsz, seq, fw)
            o = _flash(qz4, qa, kb.reshape(bsz, seq, kvw), ka, vb.reshape(bsz, seq, kvw), qz4,
                       nkv=nkv, group=grp, tq=512, tk=512, qm_lead=0, km_lead=None, v_lead=None,
                       z_lead=1, ka_shared=False, name="fox_flash")
            o = o.reshape(mp, fw)
            outs["fk_p"].append(k.reshape(bsz, seq, nkv, hd))
            outs["fv_p"].append(v.reshape(bsz, seq, nkv, hd))
            outs["fl_p"].append(lf[:, :nh].reshape(bsz, seq, nh))
            if last:
                yp = _out_final(o, None, 0, w_out, hp, gfin, tm=256)
            else:
                hp = _out_proj(o, None, 0, w_out, hp, tm=1024, tn=1024)

            qz = _proj_stack(hs, g, w_qz, (fox_scale, 1.0), tm=1024, tn=512, name="fox_qz_s")
            k, v, lf, kb, vb = _fox_kvf(hs, g, wk, wv, wf, bfp, nh=nh, tm=512)
            o = _fox_dec(page_table, qz, 0, lf, kb, vb, ck, cv, clf, j, nh=nh, nkv=nkv)
            outs["fk_s"].append(k.reshape(bd, nr, nkv, hd)[:, :ns])
            outs["fv_s"].append(v.reshape(bd, nr, nkv, hd)[:, :ns])
            outs["fl_s"].append(lf[:, :nh].reshape(bd, nr, nh)[:, :ns])
            if last:
                ys = _out_final(o, qz, 1, w_out, hs, gfin, tm=256)
            else:
                hs = _out_proj(o, qz, 1, w_out, hs, tm=512, tn=512)
        else:
            w = w_in_mla[j]
            o1, o2, o3 = ql, ql + kvl, ql + kvl + rope
            wqa, wkva = w[:, :o1].astype(BF16), w[:, o1:o2].astype(BF16)
            wkr = w[:, o2:o3]
            wkr2 = jnp.concatenate([_pad_last(wkr, LANES), _pad_last(_rot_cols(wkr), LANES)], axis=1).astype(BF16)
            wz = w[:, o3:].astype(BF16)[None]
            g = norm_mla[j].reshape(1, dm)
            qn, kvn = q_norm[j].reshape(1, ql), kv_norm[j].reshape(1, kvl)
            wqb = w_q_b[j]
            wn = wqb[:, :, :nope].reshape(ql, mh * nope).astype(BF16)
            wr = _pad_last(wqb[:, :, nope:], LANES).reshape(ql, mh * LANES).astype(BF16)
            wrr = _pad_last(_rot_cols(wqb[:, :, nope:]), LANES).reshape(ql, mh * LANES).astype(BF16)
            w_kv = jnp.stack([w_uk[j].reshape(kvl, mh * nope), w_uv[j].reshape(kvl, mh * vd)]).astype(BF16)
            w_ukt = jnp.transpose(w_uk[j], (1, 2, 0)).astype(BF16)
            w_uvh = jnp.transpose(w_uv[j], (1, 0, 2)).astype(BF16)
            w_out = w_out_mla[j].astype(BF16)

            z = _proj_stack(hp, g, wz, (1.0,), tm=1024, tn=1024, name="mla_z")
            qan, ckv, ckvb, kr, krp = _mla_a(hp, g, wqa, wkva, wkr2, qn, kvn, cos_p, sin_p, rope=rope, tm=512)
            qm, qx = _mla_q(qan, wn, wr, wrr, cos_p, sin_p, scale=mla_scale, tm=1024, hb=4)
            kvu = _proj_stack(ckvb, None, w_kv, (1.0, 1.0), tm=1024, tn=1024, name="mla_kv")
            o = _flash(qm.reshape(bsz, seq, mh * nope), qx.reshape(bsz, seq, mh * LANES),
                       kvu.reshape(2, bsz, seq, mh * nope), krp.reshape(bsz, seq, LANES),
                       kvu.reshape(2, bsz, seq, mh * vd), z.reshape(1, bsz, seq, mh * vd),
                       nkv=mh, group=1, tq=512, tk=512, qm_lead=None, km_lead=0, v_lead=1,
                       z_lead=0, ka_shared=True, name="mla_flash")
            o = o.reshape(mp, mh * vd)
            outs["mc_p"].append(ckv.reshape(bsz, seq, kvl))
            outs["mr_p"].append(kr.reshape(bsz, seq, rope))
            if last:
                yp = _out_final(o, None, 0, w_out, hp, gfin, tm=256)
            else:
                hp = _out_proj(o, None, 0, w_out, hp, tm=1024, tn=1024)

            z = _proj_stack(hs, g, wz, (1.0,), tm=1024, tn=512, name="mla_z_s")
            qan, ckv, ckvb, kr, krp = _mla_a(hs, g, wqa, wkva, wkr2, qn, kvn, cos_s, sin_s, rope=rope, tm=512)
            qm, qx = _mla_q(qan, wn, wr, wrr, cos_s, sin_s, scale=mla_scale, tm=1024, hb=4)
            qlat = _absorb(qm, w_ukt, tm=256)
            qx3 = qx.reshape(bd, nr, mh, LANES)[:, :ns].reshape(bd, ns * mh, LANES)
            olat = _mla_dec(page_table, qlat.reshape(kvl // LANES, bd, nr * mh, LANES), qx3,
                            ckvb, krp, cache_mla_ckv, ckrt, j, nh=mh, ns=ns)
            o = _unabsorb(olat.reshape(kvl // LANES, ms * mh, LANES), w_uvh, tm=128)
            outs["mc_s"].append(ckv.reshape(bd, nr, kvl)[:, :ns])
            outs["mr_s"].append(kr.reshape(bd, nr, rope)[:, :ns])
            if last:
                ys = _out_final(o, z, 0, w_out, hs, gfin, tm=256)
            else:
                hs = _out_proj(o, z, 0, w_out, hs, tm=512, tn=512)

    y_prompt = yp.reshape(bsz, seq, dm)
    y_sample = ys.reshape(bd, nr, dm)[:, :ns]
    st = lambda key: jnp.stack(outs[key])
    return (y_prompt, y_sample, st("fk_p"), st("fv_p"), st("fl_p"), st("fk_s"), st("fv_s"), st("fl_s"),
            st("mc_p"), st("mr_p"), st("mc_s"), st("mr_s"))
```
